```python
import jax
import jax.numpy as jnp
from jax import lax
import numpy as np

D_MODEL = 2048
BATCH = 2
SEQ = 16384
DEPTH = 2

GRID_W = 64
CTX_LEN = 256
D_MIX = D_MODEL
D_MLSTM = D_MIX // 2
D_POOL = D_MIX - D_MLSTM
N_HEADS = 4
HEAD_V = D_MLSTM // N_HEADS
HEAD_QK = HEAD_V // 2
CHUNK = 128
POOL_WINDOWS = (2, 4, 8, 16)
N_POOL_GROUPS = len(POOL_WINDOWS)
POOL_GROUP = D_POOL // N_POOL_GROUPS
N_GATES = 4 * N_HEADS
Q_OFF = 0
K_OFF = Q_OFF + N_HEADS * HEAD_QK
V_OFF = K_OFF + N_HEADS * HEAD_QK
O_OFF = V_OFF + D_MLSTM
U_OFF = O_OFF + D_MLSTM
GATE_OFF = U_OFF + D_POOL
IN_COLS = GATE_OFF + N_GATES
N_EXPERTS = 32
TOP_K = 4
D_FF = D_MODEL
SWIGLU_ALPHA = 1.702
SWIGLU_LIMIT = 7.0
MOE_BLOCK = 256
N_MOD = 6
EPS = 1e-6

kernel_name = 'hybrid_mlstm_pool_moe_dit'


def rmsnorm(x, g):
    xf = x.astype(jnp.float32)
    y = xf * lax.rsqrt(jnp.mean(xf * xf, axis=-1, keepdims=True) + EPS)
    return (y * g.astype(jnp.float32)).astype(x.dtype)


def modulate(h, shift, scale):
    return h * (1 + scale) + shift


def _to_heads(a, dh):
    b, t, _ = a.shape
    return a.reshape(b, t, N_HEADS, dh).transpose(0, 2, 1, 3)


def _mlstm_prep(p, b_gate):
    p = p.astype(jnp.float32)
    b, t, _ = p.shape
    q = _to_heads(p[..., Q_OFF:K_OFF], HEAD_QK) * (HEAD_QK ** -0.5)
    k = _to_heads(p[..., K_OFF:V_OFF], HEAD_QK)
    v = _to_heads(p[..., V_OFF:O_OFF], HEAD_V)
    g = p[..., GATE_OFF:].reshape(b, t, 4, N_HEADS) + b_gate.astype(jnp.float32)
    g = g.transpose(2, 0, 3, 1)
    fwd = (g[0], jax.nn.log_sigmoid(g[1]))
    bwd = (g[2], jax.nn.log_sigmoid(g[3]))
    return q, k, v, fwd, bwd


def _mlstm_scan(q, k, v, logi, logf, state):
    b, h, t, _ = q.shape
    n_chunks = t // CHUNK

    def to_chunks(a):
        a = a.reshape(a.shape[:2] + (n_chunks, CHUNK) + a.shape[3:])
        return jnp.moveaxis(a, 2, 0)

    xs = tuple(to_chunks(a) for a in (q, k, v, logi, logf))
    lower = jnp.tril(jnp.ones((CHUNK, CHUNK), dtype=bool))

    def step(carry, blk):
        C, n, m = carry
        qc, kc, vc, ic, fc = blk
        bcum = jnp.cumsum(fc, axis=-1)
        d = jnp.where(lower, bcum[..., :, None] - bcum[..., None, :] + ic[..., None, :], -jnp.inf)
        m_inter = bcum + m[..., None]
        m_t = jnp.maximum(m_inter, jnp.max(d, axis=-1))
        s = jnp.einsum('bhtd,bhsd->bhts', qc, kc) * jnp.exp(d - m_t[..., None])
        w_inter = jnp.exp(m_inter - m_t)
        num = (jnp.einsum('bhts,bhsv->bhtv', s, vc)
               + w_inter[..., None] * jnp.einsum('bhtd,bhdv->bhtv', qc, C))
        den = jnp.sum(s, axis=-1) + w_inter * jnp.einsum('bhtd,bhd->bht', qc, n)
        h_out = num / jnp.maximum(jnp.abs(den), jnp.exp(-m_t))[..., None]
        b_end = bcum[..., -1]
        g = b_end[..., None] - bcum + ic
        m_new = jnp.maximum(b_end + m, jnp.max(g, axis=-1))
        wg = jnp.exp(g - m_new[..., None])
        wc = jnp.exp(b_end + m - m_new)
        C = wc[..., None, None] * C + jnp.einsum('bhsd,bhsv->bhdv', kc * wg[..., None], vc)
        n = wc[..., None] * n + jnp.einsum('bhs,bhsd->bhd', wg, kc)
        return (C, n, m_new), h_out

    state, hs = lax.scan(step, state, xs)
    hs = jnp.moveaxis(hs, 0, 2).reshape(b, h, t, HEAD_V)
    return hs, state


def _flip(a):
    return jnp.flip(a, axis=2)


def _mlstm_bidir(q, k, v, fwd, bwd, st_f, st_b):
    h_f, st_f = _mlstm_scan(q, k, v, fwd[0], fwd[1], st_f)
    h_b, st_b = _mlstm_scan(_flip(q), _flip(k), _flip(v), _flip(bwd[0]), _flip(bwd[1]), st_b)
    return h_f + _flip(h_b), st_f, st_b


def _mlstm_out(h, o, head_gain, dtype):
    gain = head_gain.astype(jnp.float32).reshape(N_HEADS, 1, HEAD_V)
    hn = h * lax.rsqrt(jnp.mean(h * h, axis=-1, keepdims=True) + EPS) * gain
    b, _, t, _ = h.shape
    hn = hn.transpose(0, 2, 1, 3).reshape(b, t, D_MLSTM)
    return (jax.nn.sigmoid(o.astype(jnp.float32)) * hn).astype(dtype)


def mlstm_mixer(px, pc, b_gate, head_gain, with_ctx):
    b = pc.shape[0]
    zero = (jnp.zeros((b, N_HEADS, HEAD_QK, HEAD_V), jnp.float32),
            jnp.zeros((b, N_HEADS, HEAD_QK), jnp.float32),
            jnp.zeros((b, N_HEADS), jnp.float32))
    qc, kc, vc, fc, bc = _mlstm_prep(pc, b_gate)
    h_c, st_f, st_b = _mlstm_bidir(qc, kc, vc, fc, bc, zero, zero)
    qx, kx, vx, fx, bx = _mlstm_prep(px, b_gate)
    h_x, _, _ = _mlstm_bidir(qx, kx, vx, fx, bx, st_f, st_b)
    y_x = _mlstm_out(h_x, px[..., O_OFF:U_OFF], head_gain, px.dtype)
    y_c = _mlstm_out(h_c, pc[..., O_OFF:U_OFF], head_gain, pc.dtype) if with_ctx else None
    return y_x, y_c


def pool_mixer(u, w_pool, pool_scale):
    bp, n, _ = u.shape
    uf = u.astype(jnp.float32).reshape(bp, n, N_POOL_GROUPS, POOL_GROUP)
    cs = jnp.concatenate([jnp.zeros((bp, 1, N_POOL_GROUPS, POOL_GROUP), jnp.float32),
                          jnp.cumsum(uf, axis=1)], axis=1)
    pos = jnp.arange(n)
    outs = []
    for g, w in enumerate(POOL_WINDOWS):
        left = w // 2
        right = w - 1 - left
        lo = jnp.clip(pos - left, 0, n)
        hi = jnp.clip(pos + right + 1, 0, n)
        cs_g = cs[:, :, g]
        mean = (cs_g[:, hi] - cs_g[:, lo]) / (hi - lo).astype(jnp.float32)[:, None]
        outs.append(mean - uf[:, :, g])
    d = jnp.stack(outs, axis=2)
    y = jnp.einsum('bngc,gcd->bngd', d, w_pool.astype(jnp.float32))
    y = y * pool_scale.astype(jnp.float32).reshape(N_POOL_GROUPS, POOL_GROUP)
    return y.reshape(bp, n, D_POOL).astype(u.dtype)


def moe(h, router_w, router_b, w_gate_up, b_gate_up, w_down, b_down):
    t, d = h.shape
    logits = h.astype(jnp.float32) @ router_w.astype(jnp.float32) + router_b.astype(jnp.float32)
    top_val, top_idx = lax.top_k(logits, TOP_K)
    gates = jax.nn.softmax(top_val, axis=-1)
    n_assign = t * TOP_K
    flat_e = top_idx.reshape(-1)
    flat_tok = (jnp.arange(n_assign) // TOP_K).astype(jnp.int32)
    order = jnp.argsort(flat_e)
    e_sorted = flat_e[order]
    counts = jnp.bincount(flat_e, length=N_EXPERTS)
    padded = (counts + MOE_BLOCK - 1) // MOE_BLOCK * MOE_BLOCK
    start = jnp.cumsum(counts) - counts
    ends_p = jnp.cumsum(padded)
    start_p = ends_p - padded
    dest = start_p[e_sorted] + jnp.arange(n_assign) - start[e_sorted]
    n_blocks = -(-n_assign // MOE_BLOCK) + N_EXPERTS
    slot_tok = jnp.full((n_blocks * MOE_BLOCK,), t, jnp.int32).at[dest].set(flat_tok[order])
    slot_gate = jnp.zeros((n_blocks * MOE_BLOCK,), jnp.float32).at[dest].set(gates.reshape(-1)[order])
    block_e = jnp.minimum(jnp.searchsorted(ends_p, jnp.arange(n_blocks) * MOE_BLOCK, side='right'),
                          N_EXPERTS - 1)
    h_pad = jnp.concatenate([h, jnp.zeros((1, d), h.dtype)], axis=0)

    def body(y, blk):
        tok, gate, e = blk
        xb = h_pad[tok]
        gu = xb @ w_gate_up[e] + b_gate_up[e]
        x_glu = jnp.minimum(gu[:, :D_FF], SWIGLU_LIMIT)
        x_lin = jnp.clip(gu[:, D_FF:], -SWIGLU_LIMIT, SWIGLU_LIMIT)
        act = x_glu * jax.nn.sigmoid(SWIGLU_ALPHA * x_glu) * (x_lin + 1)
        out = act @ w_down[e] + b_down[e]
        return y.at[tok].add(out * gate[:, None].astype(out.dtype)), None

    y, _ = lax.scan(body, jnp.zeros((t + 1, d), h.dtype),
                    (slot_tok.reshape(n_blocks, MOE_BLOCK), slot_gate.reshape(n_blocks, MOE_BLOCK), block_e))
    return y[:t]


def setup_inputs(seed: int = 0) -> dict:
    key = jax.random.key(seed)
    ks = jax.random.split(key, 21)
    f32 = jnp.float32

    def nrm(k, shape, scale):
        return jax.random.normal(k, shape, f32) * scale

    gate_base = jnp.array([0.0, 3.0, 0.0, 3.0], f32)[None, :, None]
    return {
        'x': nrm(ks[0], (BATCH, SEQ, D_MODEL), 1.0),
        'c': nrm(ks[1], (BATCH, D_MODEL), 1.0),
        'ctx': nrm(ks[2], (BATCH, CTX_LEN, D_MODEL), 1.0),
        'c_ctx': nrm(ks[3], (D_MODEL,), 1.0),
        'w_mod': nrm(ks[4], (DEPTH, D_MODEL, N_MOD * D_MODEL), 0.5 * D_MODEL ** -0.5),
        'b_mod': nrm(ks[5], (DEPTH, N_MOD * D_MODEL), 0.02),
        'norm_mix': 1.0 + nrm(ks[6], (DEPTH, D_MODEL), 0.05),
        'w_in': nrm(ks[7], (DEPTH, D_MODEL, IN_COLS), D_MODEL ** -0.5),
        'b_gate': gate_base + nrm(ks[8], (DEPTH, 4, N_HEADS), 0.5),
        'head_gain': 1.0 + nrm(ks[9], (DEPTH, D_MLSTM), 0.05),
        'w_pool': nrm(ks[10], (DEPTH, N_POOL_GROUPS, POOL_GROUP, POOL_GROUP), POOL_GROUP ** -0.5),
        'pool_scale': 1.0 + nrm(ks[11], (DEPTH, D_POOL), 0.1),
        'w_out': nrm(ks[12], (DEPTH, D_MIX, D_MODEL), D_MIX ** -0.5),
        'norm_ffn': 1.0 + nrm(ks[13], (DEPTH, D_MODEL), 0.05),
        'router_w': nrm(ks[14], (DEPTH, D_MODEL, N_EXPERTS), D_MODEL ** -0.5),
        'router_b': nrm(ks[15], (DEPTH, N_EXPERTS), 0.01),
        'w_gate_up': nrm(ks[16], (DEPTH, N_EXPERTS, D_MODEL, 2 * D_FF), D_MODEL ** -0.5),
        'b_gate_up': nrm(ks[17], (DEPTH, N_EXPERTS, 2 * D_FF), 0.02),
        'w_down': nrm(ks[18], (DEPTH, N_EXPERTS, D_FF, D_MODEL), D_FF ** -0.5),
        'b_down': nrm(ks[19], (DEPTH, N_EXPERTS, D_MODEL), 0.02),
        'norm_final': 1.0 + nrm(ks[20], (D_MODEL,), 0.05),
    }


def reference(x, c, ctx, c_ctx, w_mod, b_mod, norm_mix, w_in, b_gate, head_gain, w_pool,
              pool_scale, w_out, norm_ffn, router_w, router_b, w_gate_up, b_gate_up,
              w_down, b_down, norm_final):
    b, s, d = x.shape
    ctx_len = ctx.shape[1]
    rows = s // GRID_W
    xc = ctx
    for l in range(DEPTH):
        last = l == DEPTH - 1
        mod_x = (jax.nn.silu(c) @ w_mod[l] + b_mod[l])[:, None, :]
        mod_c = jax.nn.silu(c_ctx) @ w_mod[l] + b_mod[l]
        sh1, sc1, g1, sh2, sc2, g2 = jnp.split(mod_x, N_MOD, axis=-1)
        sh1c, sc1c, g1c, sh2c, sc2c, g2c = jnp.split(mod_c, N_MOD, axis=-1)

        hx = modulate(rmsnorm(x, norm_mix[l]), sh1, sc1)
        hc = modulate(rmsnorm(xc, norm_mix[l]), sh1c, sc1c)
        px = hx @ w_in[l]
        pc = hc @ w_in[l]
        m_x, m_c = mlstm_mixer(px, pc, b_gate[l], head_gain[l], not last)
        pool_x = pool_mixer(px[..., U_OFF:GATE_OFF].reshape(b * rows, GRID_W, D_POOL),
                            w_pool[l], pool_scale[l]).reshape(b, s, D_POOL)
        x = x + g1 * (jnp.concatenate([m_x, pool_x], axis=-1) @ w_out[l])
        hx2 = modulate(rmsnorm(x, norm_ffn[l]), sh2, sc2)

        if last:
            f = moe(hx2.reshape(b * s, d), router_w[l], router_b[l], w_gate_up[l],
                    b_gate_up[l], w_down[l], b_down[l])
            x = x + g2 * f.reshape(b, s, d)
        else:
            pool_c = pool_mixer(pc[..., U_OFF:GATE_OFF], w_pool[l], pool_scale[l])
            xc = xc + g1c * (jnp.concatenate([m_c, pool_c], axis=-1) @ w_out[l])
            hc2 = modulate(rmsnorm(xc, norm_ffn[l]), sh2c, sc2c)
            tokens = jnp.concatenate([hx2.reshape(b * s, d), hc2.reshape(b * ctx_len, d)], axis=0)
            f = moe(tokens, router_w[l], router_b[l], w_gate_up[l], b_gate_up[l],
                    w_down[l], b_down[l])
            x = x + g2 * f[:b * s].reshape(b, s, d)
            xc = xc + g2c * f[b * s:].reshape(b, ctx_len, d)
    return rmsnorm(x, norm_final)
```

```python
import functools

import jax
import jax.numpy as jnp
from jax import lax
from jax.experimental import pallas as pl
from jax.experimental.pallas import tpu as pltpu

F32 = jnp.float32
BF16 = jnp.bfloat16
U32 = jnp.uint32
I32 = jnp.int32

GRID_W = 64
N_HEADS = 4
HEAD_V = 256
HEAD_QK = 128
CHUNK = 128
POOL_WINDOWS = (2, 4, 8, 16)
N_POOL_GROUPS = 4
POOL_GROUP = 256
N_EXPERTS = 32
TOP_K = 4
SWIGLU_ALPHA = 1.702
SWIGLU_LIMIT = 7.0
N_MOD = 6
EPS = 1e-6
LANES = 128
MOD_ROWS = 8

ROW_TILE = 512
MOE_ROWS = 512
MOE_FF_TILE = 512
DISPATCH_ROWS = 256
COMBINE_ROWS = 128
VMEM_LIMIT = 56 * 1024 * 1024


def _cparams(sem):
    return pltpu.CompilerParams(dimension_semantics=sem, vmem_limit_bytes=VMEM_LIMIT)


def _pack_bf16_pair(lo, hi):
    lo_bits = pltpu.bitcast(lo.astype(BF16).astype(F32), U32) >> 16
    hi_bits = pltpu.bitcast(hi.astype(BF16).astype(F32), U32) & jnp.uint32(0xFFFF0000)
    return hi_bits | lo_bits


def _unpack_bf16_pair(w):
    lo = pltpu.bitcast(w << 16, F32)
    hi = pltpu.bitcast(w & jnp.uint32(0xFFFF0000), F32)
    return lo, hi


def _mod_kernel(c_ref, w_ref, b_ref, o_ref):
    c = c_ref[...]
    s = c * jax.nn.sigmoid(c)
    o_ref[...] = jnp.dot(s, w_ref[...], preferred_element_type=F32,
                         precision=lax.Precision.HIGHEST) + b_ref[...]


def _modulation(c_rows, w_mod, b_mod):
    depth, d, n = w_mod.shape
    tn = 1024
    return pl.pallas_call(
        _mod_kernel,
        grid=(depth, n // tn),
        in_specs=[
            pl.BlockSpec((MOD_ROWS, d), lambda l, j: (0, 0)),
            pl.BlockSpec((None, d, tn), lambda l, j: (l, 0, j)),
            pl.BlockSpec((None, 1, tn), lambda l, j: (l, 0, j)),
        ],
        out_specs=pl.BlockSpec((None, MOD_ROWS, tn), lambda l, j: (l, 0, j)),
        out_shape=jax.ShapeDtypeStruct((depth, MOD_ROWS, n), F32),
        compiler_params=_cparams(("arbitrary", "arbitrary")),
        name="modulation",
    )(c_rows, w_mod, b_mod.reshape(depth, 1, n))


def _in_proj_kernel(x_ref, sh_ref, sc_ref, g_ref, w_ref, wg_ref, p_ref, gate_ref, h_scr):
    @pl.when(pl.program_id(1) == 0)
    def _():
        x = x_ref[...]
        y = x * lax.rsqrt(jnp.mean(x * x, axis=-1, keepdims=True) + EPS) * g_ref[...]
        h = (y * (1.0 + sc_ref[...]) + sh_ref[...]).astype(BF16)
        h_scr[...] = h
        gate_ref[...] = jnp.dot(h, wg_ref[...], preferred_element_type=F32)

    p_ref[...] = jnp.dot(h_scr[...], w_ref[...], preferred_element_type=F32).astype(BF16)


def _in_proj(x_all, mods, layer, group_of_tile, norm_g, w_main, w_gate):
    t_all, d = x_all.shape
    n_main = w_main.shape[1]
    tn = 2048
    mod_spec = lambda k: pl.BlockSpec(
        (None, None, 1, d), lambda i, j: (layer, group_of_tile(i), 0, k))
    return pl.pallas_call(
        _in_proj_kernel,
        grid=(t_all // ROW_TILE, n_main // tn),
        in_specs=[
            pl.BlockSpec((ROW_TILE, d), lambda i, j: (i, 0)),
            mod_spec(0), mod_spec(1),
            pl.BlockSpec((1, d), lambda i, j: (0, 0)),
            pl.BlockSpec((d, tn), lambda i, j: (0, j)),
            pl.BlockSpec((d, LANES), lambda i, j: (0, 0)),
        ],
        out_specs=[
            pl.BlockSpec((ROW_TILE, tn), lambda i, j: (i, j)),
            pl.BlockSpec((ROW_TILE, LANES), lambda i, j: (i, 0)),
        ],
        out_shape=[
            jax.ShapeDtypeStruct((t_all, n_main), BF16),
            jax.ShapeDtypeStruct((t_all, LANES), F32),
        ],
        scratch_shapes=[pltpu.VMEM((ROW_TILE, d), BF16)],
        compiler_params=_cparams(("arbitrary", "arbitrary")),
        name="in_proj",
    )(x_all, mods, mods, norm_g, w_main, w_gate)


def _log_sigmoid(x):
    return jnp.minimum(x, 0.0) - jnp.log(1.0 + jnp.exp(-jnp.abs(x)))


def _mlstm_chain(q, k, v, a, a_t, cs, cs_t, col_i, col_f, reverse, c_ref, n_ref, m_ref):
    bc = cs[:, col_f:col_f + 1]
    br = cs_t[col_f:col_f + 1, :]
    ir = a_t[col_i:col_i + 1, :]
    ic = a[:, col_i:col_i + 1]
    t_idx = lax.broadcasted_iota(I32, (CHUNK, CHUNK), 0)
    s_idx = lax.broadcasted_iota(I32, (CHUNK, CHUNK), 1)
    mask = (s_idx >= t_idx) if reverse else (s_idx <= t_idx)
    dmat = jnp.where(mask, bc - br + ir, -jnp.inf)
    m_prev = m_ref[...]
    m_inter = bc + m_prev
    m_t = jnp.maximum(m_inter, jnp.max(dmat, axis=-1, keepdims=True))
    scale = HEAD_QK ** -0.5
    qk = lax.dot_general(q, k, (((1,), (1,)), ((), ())), preferred_element_type=F32)
    s = qk * (scale * jnp.exp(dmat - m_t))
    w_inter = scale * jnp.exp(m_inter - m_t)
    c_old = c_ref[...]
    num = (jnp.dot(s.astype(BF16), v, preferred_element_type=F32)
           + w_inter * jnp.dot(q, c_old.astype(BF16), preferred_element_type=F32))
    qn = jnp.sum(q.astype(F32) * n_ref[...], axis=-1, keepdims=True)
    den = jnp.sum(s, axis=-1, keepdims=True) + w_inter * qn
    h = num / jnp.maximum(jnp.abs(den), jnp.exp(-m_t))

    b_end = bc[0:1, :] if reverse else bc[CHUNK - 1:CHUNK, :]
    g = b_end - bc + ic
    m_new = jnp.maximum(b_end + m_prev, jnp.max(g, axis=0, keepdims=True))
    wg = jnp.exp(g - m_new)
    wc = jnp.exp(b_end + m_prev - m_new)
    kw = k.astype(F32) * wg
    c_ref[...] = wc * c_old + lax.dot_general(
        kw.astype(BF16), v, (((0,), (0,)), ((), ())), preferred_element_type=F32)
    n_ref[...] = wc * n_ref[...] + jnp.sum(kw, axis=0, keepdims=True)
    m_ref[...] = m_new
    return h


def _mlstm_kernel(qf_ref, kf_ref, vf_ref, gf_ref, qb_ref, kb_ref, vb_ref, gb_ref, bias_ref,
                  hf_ref, hb_ref, c_scr, n_scr, m_scr):
    @pl.when(pl.program_id(1) == 0)
    def _():
        c_scr[...] = jnp.zeros_like(c_scr)
        n_scr[...] = jnp.zeros_like(n_scr)
        m_scr[...] = jnp.zeros_like(m_scr)

    lane = lax.broadcasted_iota(I32, (CHUNK, LANES), 1)
    is_f = ((lane >= N_HEADS) & (lane < 2 * N_HEADS)) | ((lane >= 3 * N_HEADS) & (lane < 4 * N_HEADS))
    t_idx = lax.broadcasted_iota(I32, (CHUNK, CHUNK), 0)
    s_idx = lax.broadcasted_iota(I32, (CHUNK, CHUNK), 1)

    for direction, (q_ref, k_ref, v_ref, g_ref, out_ref) in enumerate(
            ((qf_ref, kf_ref, vf_ref, gf_ref, hf_ref), (qb_ref, kb_ref, vb_ref, gb_ref, hb_ref))):
        reverse = direction == 1
        graw = g_ref[...] + bias_ref[...]
        a = jnp.where(is_f, _log_sigmoid(graw), graw)
        tri = ((s_idx >= t_idx) if reverse else (s_idx <= t_idx)).astype(F32)
        cs = jnp.dot(tri, a, preferred_element_type=F32, precision=lax.Precision.HIGHEST)
        a_t = a.T
        cs_t = cs.T
        for hd in range(N_HEADS):
            col_i = 2 * direction * N_HEADS + hd
            col_f = col_i + N_HEADS
            h = _mlstm_chain(
                q_ref[:, hd * HEAD_QK:(hd + 1) * HEAD_QK],
                k_ref[:, hd * HEAD_QK:(hd + 1) * HEAD_QK],
                v_ref[:, hd * HEAD_V:(hd + 1) * HEAD_V],
                a, a_t, cs, cs_t, col_i, col_f, reverse,
                c_scr.at[direction, hd], n_scr.at[direction, hd], m_scr.at[direction, hd])
            out_ref[:, hd * HEAD_V:(hd + 1) * HEAD_V] = h.astype(BF16)


def _mlstm(p_main, gates, bias_row, batch, seq, ctx_len):
    t_all = p_main.shape[0]
    n_ctx = ctx_len // CHUNK
    n_seq = seq // CHUNK
    ctx_base = batch * seq // CHUNK
    d_qk = N_HEADS * HEAD_QK
    d_v = N_HEADS * HEAD_V

    def fwd_blk(b, i):
        return jnp.where(i < n_ctx, ctx_base + b * n_ctx + i, b * n_seq + (i - n_ctx))

    def bwd_blk(b, i):
        return jnp.where(i < n_ctx, ctx_base + b * n_ctx + (n_ctx - 1 - i),
                         b * n_seq + (n_seq - 1 - (i - n_ctx)))

    def specs(blk):
        return [
            pl.BlockSpec((CHUNK, d_qk), lambda b, i: (blk(b, i), 0)),
            pl.BlockSpec((CHUNK, d_qk), lambda b, i: (blk(b, i), 1)),
            pl.BlockSpec((CHUNK, d_v), lambda b, i: (blk(b, i), 1)),
            pl.BlockSpec((CHUNK, LANES), lambda b, i: (blk(b, i), 0)),
        ]

    return pl.pallas_call(
        _mlstm_kernel,
        grid=(batch, n_ctx + n_seq),
        in_specs=specs(fwd_blk) + specs(bwd_blk) + [pl.BlockSpec((1, LANES), lambda b, i: (0, 0))],
        out_specs=[
            pl.BlockSpec((CHUNK, d_v), lambda b, i: (fwd_blk(b, i), 0)),
            pl.BlockSpec((CHUNK, d_v), lambda b, i: (bwd_blk(b, i), 0)),
        ],
        out_shape=[jax.ShapeDtypeStruct((t_all, d_v), BF16)] * 2,
        scratch_shapes=[
            pltpu.VMEM((2, N_HEADS, HEAD_QK, HEAD_V), F32),
            pltpu.VMEM((2, N_HEADS, 1, HEAD_QK), F32),
            pltpu.VMEM((2, N_HEADS, 1, 1), F32),
        ],
        compiler_params=_cparams(("arbitrary", "arbitrary")),
        name="mlstm",
    )(p_main, p_main, p_main, gates, p_main, p_main, p_main, gates, bias_row)


def _mix_out_kernel(x_ref, o_ref, u_ref, hf_ref, hb_ref, gain_ref, pmat_ref, wpool_ref, pscale_ref,
                    wout_ref, g1_ref, sh2_ref, sc2_ref, nffn_ref, rw_ref, rb_ref,
                    xnew_ref, h2_ref, logit_ref):
    acc = None
    for hd in range(N_HEADS):
        sl = slice(hd * HEAD_V, (hd + 1) * HEAD_V)
        h = hf_ref[:, sl].astype(F32) + hb_ref[:, sl].astype(F32)
        hn = h * lax.rsqrt(jnp.mean(h * h, axis=-1, keepdims=True) + EPS) * gain_ref[:, sl]
        m = (jax.nn.sigmoid(o_ref[:, sl].astype(F32)) * hn).astype(BF16)
        part = jnp.dot(m, wout_ref[sl, :], preferred_element_type=F32)
        acc = part if acc is None else acc + part
    d_mlstm = N_HEADS * HEAD_V
    for g in range(N_POOL_GROUPS):
        sl = slice(g * POOL_GROUP, (g + 1) * POOL_GROUP)
        u = u_ref[:, sl]
        dlt = jnp.dot(pmat_ref[g], u, preferred_element_type=F32) - u.astype(F32)
        y = jnp.dot(dlt.astype(BF16), wpool_ref[g], preferred_element_type=F32) * pscale_ref[:, sl]
        acc = acc + jnp.dot(y.astype(BF16), wout_ref[d_mlstm + g * POOL_GROUP:d_mlstm + (g + 1) * POOL_GROUP, :],
                            preferred_element_type=F32)
    xn = x_ref[...] + g1_ref[...] * acc
    xnew_ref[...] = xn
    y2 = xn * lax.rsqrt(jnp.mean(xn * xn, axis=-1, keepdims=True) + EPS) * nffn_ref[...]
    h2 = y2 * (1.0 + sc2_ref[...]) + sh2_ref[...]
    half = h2.shape[1] // 2
    h2_ref[...] = _pack_bf16_pair(h2[:, :half], h2[:, half:])
    logit_ref[...] = jnp.dot(h2.astype(BF16), rw_ref[...], preferred_element_type=F32) + rb_ref[...]


def _mix_out(x_all, p_main, hf, hb, mods, layer, group_of_tile, tile_kind, gain, pmats, w_pool, pool_scale,
             w_out, norm_ffn, router_w, router_b):
    t_all, d = x_all.shape
    d_half = d // 2
    mod_spec = lambda k: pl.BlockSpec(
        (None, None, 1, d), lambda i: (layer, group_of_tile(i), 0, k))
    const = lambda shape: pl.BlockSpec(shape, lambda i: (0,) * len(shape), pipeline_mode=pl.Buffered(1))
    return pl.pallas_call(
        _mix_out_kernel,
        grid=(t_all // ROW_TILE,),
        in_specs=[
            pl.BlockSpec((ROW_TILE, d), lambda i: (i, 0)),
            pl.BlockSpec((ROW_TILE, d_half), lambda i: (i, 2)),
            pl.BlockSpec((ROW_TILE, d_half), lambda i: (i, 3)),
            pl.BlockSpec((ROW_TILE, d_half), lambda i: (i, 0)),
            pl.BlockSpec((ROW_TILE, d_half), lambda i: (i, 0)),
            const((1, d_half)),
            pl.BlockSpec((None, N_POOL_GROUPS, ROW_TILE, ROW_TILE), lambda i: (tile_kind(i), 0, 0, 0)),
            const((N_POOL_GROUPS, POOL_GROUP, POOL_GROUP)),
            const((1, d_half)),
            const((d, d)),
            mod_spec(2), mod_spec(3), mod_spec(4),
            const((1, d)),
            const((d, LANES)),
            const((1, LANES)),
        ],
        out_specs=[
            pl.BlockSpec((ROW_TILE, d), lambda i: (i, 0)),
            pl.BlockSpec((ROW_TILE, d_half), lambda i: (i, 0)),
            pl.BlockSpec((ROW_TILE, LANES), lambda i: (i, 0)),
        ],
        out_shape=[
            jax.ShapeDtypeStruct((t_all, d), F32),
            jax.ShapeDtypeStruct((t_all, d_half), U32),
            jax.ShapeDtypeStruct((t_all, LANES), F32),
        ],
        compiler_params=_cparams(("arbitrary",)),
        name="mix_out",
    )(x_all, p_main, p_main, hf, hb, gain, pmats, w_pool, pool_scale, w_out,
      mods, mods, mods, norm_ffn, router_w, router_b)


def _route_kernel(logit_ref, idx_ref, gate_ref):
    lg = logit_ref[...]
    lane = lax.broadcasted_iota(I32, lg.shape, 1).astype(F32)
    lg = jnp.where(lane < N_EXPERTS, lg, -jnp.inf)
    vals, idxs = [], []
    for _ in range(TOP_K):
        mx = jnp.max(lg, axis=-1, keepdims=True)
        ix = jnp.min(jnp.where(lg == mx, lane, float(LANES)), axis=-1, keepdims=True)
        vals.append(mx)
        idxs.append(ix)
        lg = jnp.where(lane == ix, -jnp.inf, lg)
    es = [jnp.exp(v - vals[0]) for v in vals]
    tot = es[0] + es[1] + es[2] + es[3]
    idx_out = jnp.zeros(lg.shape, F32)
    gate_out = jnp.zeros(lg.shape, F32)
    for k in range(TOP_K):
        idx_out = jnp.where(lane == k, idxs[k], idx_out)
        gate_out = jnp.where(lane == k, es[k] / tot, gate_out)
    idx_ref[...] = idx_out.astype(I32)
    gate_ref[...] = gate_out


def _route(logits):
    t = logits.shape[0]
    spec = pl.BlockSpec((ROW_TILE, LANES), lambda i: (i, 0))
    return pl.pallas_call(
        _route_kernel,
        grid=(t // ROW_TILE,),
        in_specs=[spec],
        out_specs=[spec, spec],
        out_shape=[jax.ShapeDtypeStruct((t, LANES), I32), jax.ShapeDtypeStruct((t, LANES), F32)],
        compiler_params=_cparams(("arbitrary",)),
        name="route",
    )(logits)


def _dispatch_kernel(dest_ref, h_hbm, slots_in, slots_hbm, sem):
    del slots_in
    base = pl.program_id(0) * DISPATCH_ROWS

    def row_copy(src_row, dst_row):
        return pltpu.make_async_copy(h_hbm.at[pl.ds(src_row, 1)], slots_hbm.at[pl.ds(dst_row, 1)], sem)

    def issue(t, carry):
        for k in range(TOP_K):
            row_copy(base + t, dest_ref[t * TOP_K + k]).start()
        return carry

    lax.fori_loop(0, DISPATCH_ROWS, issue, 0)

    def drain(t, carry):
        for k in range(TOP_K):
            row_copy(base + t, dest_ref[t * TOP_K + k]).wait()
        return carry

    lax.fori_loop(0, DISPATCH_ROWS, drain, 0)


def _dispatch(dest_flat, h_packed, n_tokens, n_slots):
    width = h_packed.shape[1]
    slots0 = jnp.zeros((n_slots, width), U32)
    return pl.pallas_call(
        _dispatch_kernel,
        grid=(n_tokens // DISPATCH_ROWS,),
        in_specs=[
            pl.BlockSpec((DISPATCH_ROWS * TOP_K,), lambda i: (i,), memory_space=pltpu.SMEM),
            pl.BlockSpec(memory_space=pl.ANY),
            pl.BlockSpec(memory_space=pl.ANY),
        ],
        out_specs=pl.BlockSpec(memory_space=pl.ANY),
        out_shape=jax.ShapeDtypeStruct((n_slots, width), U32),
        scratch_shapes=[pltpu.SemaphoreType.DMA(())],
        input_output_aliases={2: 0},
        compiler_params=_cparams(("arbitrary",)),
        name="dispatch",
    )(dest_flat, h_packed, slots0)


def _moe_kernel(be_ref, na_ref, x_ref, wg_ref, wu_ref, bg_ref, bu_ref, wd_ref, bd_ref, o_ref,
                x_scr, acc_scr):
    i = pl.program_id(0)
    j = pl.program_id(1)
    n_j = pl.num_programs(1)
    half = x_ref.shape[1]

    @pl.when(i < na_ref[0])
    def _():
        @pl.when(j == 0)
        def _():
            lo, hi = _unpack_bf16_pair(x_ref[...])
            x_scr[:, :half] = lo.astype(BF16)
            x_scr[:, half:] = hi.astype(BF16)

        x = x_scr[...]
        gl = jnp.dot(x, wg_ref[...].astype(BF16), preferred_element_type=F32) + bg_ref[...]
        ul = jnp.dot(x, wu_ref[...].astype(BF16), preferred_element_type=F32) + bu_ref[...]
        x_glu = jnp.minimum(gl, SWIGLU_LIMIT)
        x_lin = jnp.clip(ul, -SWIGLU_LIMIT, SWIGLU_LIMIT)
        act = x_glu * jax.nn.sigmoid(SWIGLU_ALPHA * x_glu) * (x_lin + 1.0)
        part = jnp.dot(act.astype(BF16), wd_ref[...].astype(BF16), preferred_element_type=F32)

        @pl.when(j == 0)
        def _():
            acc_scr[...] = part + bd_ref[...]

        @pl.when(j > 0)
        def _():
            acc_scr[...] += part

        @pl.when(j == n_j - 1)
        def _():
            y = acc_scr[...]
            o_ref[...] = _pack_bf16_pair(y[:, :half], y[:, half:])


def _moe(slots, block_e, n_active, layer, w_gate_up, b_gate_up, w_down, b_down):
    n_slots, half = slots.shape
    d = 2 * half
    depth, n_e, _, two_f = w_gate_up.shape
    d_ff = two_f // 2
    n_j = d_ff // MOE_FF_TILE
    n_blocks = n_slots // MOE_ROWS

    def blk(i, na):
        return jnp.minimum(i, na[0] - 1)

    def jj(i, j, na):
        return jnp.where(i < na[0], j, n_j - 1)

    grid_spec = pltpu.PrefetchScalarGridSpec(
        num_scalar_prefetch=2,
        grid=(n_blocks, n_j),
        in_specs=[
            pl.BlockSpec((MOE_ROWS, half), lambda i, j, be, na: (blk(i, na), 0)),
            pl.BlockSpec((None, None, d, MOE_FF_TILE),
                         lambda i, j, be, na: (layer, be[blk(i, na)], 0, jj(i, j, na))),
            pl.BlockSpec((None, None, d, MOE_FF_TILE),
                         lambda i, j, be, na: (layer, be[blk(i, na)], 0, n_j + jj(i, j, na))),
            pl.BlockSpec((None, None, 1, MOE_FF_TILE),
                         lambda i, j, be, na: (layer, be[blk(i, na)], 0, jj(i, j, na))),
            pl.BlockSpec((None, None, 1, MOE_FF_TILE),
                         lambda i, j, be, na: (layer, be[blk(i, na)], 0, n_j + jj(i, j, na))),
            pl.BlockSpec((None, None, MOE_FF_TILE, d),
                         lambda i, j, be, na: (layer, be[blk(i, na)], jj(i, j, na), 0)),
            pl.BlockSpec((None, None, 1, d), lambda i, j, be, na: (layer, be[blk(i, na)], 0, 0)),
        ],
        out_specs=pl.BlockSpec((MOE_ROWS, half), lambda i, j, be, na: (blk(i, na), 0)),
        scratch_shapes=[pltpu.VMEM((MOE_ROWS, d), BF16), pltpu.VMEM((MOE_ROWS, d), F32)],
    )
    return pl.pallas_call(
        _moe_kernel,
        grid_spec=grid_spec,
        out_shape=jax.ShapeDtypeStruct((n_slots, half), U32),
        compiler_params=_cparams(("arbitrary", "arbitrary")),
        name="moe",
    )(block_e, n_active, slots, w_gate_up, w_gate_up,
      b_gate_up.reshape(depth, n_e, 1, two_f), b_gate_up.reshape(depth, n_e, 1, two_f),
      w_down, b_down.reshape(depth, n_e, 1, d))


def _combine_kernel(final_norm, dest_ref, gate_ref, x_ref, g2_ref, nf_ref, y_hbm, o_ref, rows, sem):
    def row_copy(t, k):
        return pltpu.make_async_copy(
            y_hbm.at[pl.ds(dest_ref[t * TOP_K + k], 1)], rows.at[k, pl.ds(t, 1)], sem)

    def issue(t, carry):
        for k in range(TOP_K):
            row_copy(t, k).start()
        return carry

    lax.fori_loop(0, COMBINE_ROWS, issue, 0)

    def drain(t, carry):
        for k in range(TOP_K):
            row_copy(t, k).wait()
        return carry

    lax.fori_loop(0, COMBINE_ROWS, drain, 0)

    half = rows.shape[2]
    gates = gate_ref[...]
    f_lo = None
    f_hi = None
    for k in range(TOP_K):
        lo, hi = _unpack_bf16_pair(rows[k])
        gk = gates[:, k:k + 1]
        f_lo = gk * lo if f_lo is None else f_lo + gk * lo
        f_hi = gk * hi if f_hi is None else f_hi + gk * hi
    y_lo = x_ref[:, :half] + g2_ref[:, :half] * f_lo
    y_hi = x_ref[:, half:] + g2_ref[:, half:] * f_hi
    if final_norm:
        ms = (jnp.sum(y_lo * y_lo, axis=-1, keepdims=True)
              + jnp.sum(y_hi * y_hi, axis=-1, keepdims=True)) / (2 * half)
        r = lax.rsqrt(ms + EPS)
        y_lo = y_lo * r * nf_ref[:, :half]
        y_hi = y_hi * r * nf_ref[:, half:]
    o_ref[:, :half] = y_lo
    o_ref[:, half:] = y_hi


def _combine(dest_flat, gates, x_all, mods, layer, group_of_tile_c, y_slots, n_tokens, norm_final, final_norm):
    d = x_all.shape[1]
    half = d // 2
    return pl.pallas_call(
        functools.partial(_combine_kernel, final_norm),
        grid=(n_tokens // COMBINE_ROWS,),
        in_specs=[
            pl.BlockSpec((COMBINE_ROWS * TOP_K,), lambda i: (i,), memory_space=pltpu.SMEM),
            pl.BlockSpec((COMBINE_ROWS, LANES), lambda i: (i, 0)),
            pl.BlockSpec((COMBINE_ROWS, d), lambda i: (i, 0)),
            pl.BlockSpec((None, None, 1, d), lambda i: (layer, group_of_tile_c(i), 0, 5)),
            pl.BlockSpec((1, d), lambda i: (0, 0)),
            pl.BlockSpec(memory_space=pl.ANY),
        ],
        out_specs=pl.BlockSpec((COMBINE_ROWS, d), lambda i: (i, 0)),
        out_shape=jax.ShapeDtypeStruct((n_tokens, d), F32),
        scratch_shapes=[pltpu.VMEM((TOP_K, COMBINE_ROWS, half), U32), pltpu.SemaphoreType.DMA(())],
        compiler_params=_cparams(("arbitrary",)),
        name="combine",
    )(dest_flat, gates, x_all, mods, norm_final, y_slots)


def _pool_matrices(seg_len, tile):
    pos = jnp.arange(tile)
    seg = pos // seg_len
    p = pos % seg_len
    mats = []
    for w in POOL_WINDOWS:
        left = w // 2
        right = w - 1 - left
        lo = jnp.clip(p - left, 0, seg_len)
        hi = jnp.clip(p + right + 1, 0, seg_len)
        inside = ((seg[:, None] == seg[None, :]) & (p[None, :] >= lo[:, None]) & (p[None, :] < hi[:, None]))
        mats.append(jnp.where(inside, 1.0 / (hi - lo).astype(F32)[:, None], 0.0))
    return jnp.stack(mats).astype(BF16)


def _routing_plan(top_idx, n_blocks):
    flat_e = top_idx.reshape(-1)
    onehot = (flat_e[:, None] == jnp.arange(N_EXPERTS, dtype=I32)[None, :]).astype(I32)
    csum = jnp.cumsum(onehot, axis=0)
    rank = jnp.take_along_axis(csum, flat_e[:, None], axis=1)[:, 0] - 1
    counts = csum[-1]
    blocks_e = (counts + MOE_ROWS - 1) // MOE_ROWS
    blk_end = jnp.cumsum(blocks_e)
    start_p = (blk_end - blocks_e) * MOE_ROWS
    dest = (start_p[flat_e] + rank).astype(I32)
    block_e = jnp.minimum(
        jnp.searchsorted(blk_end, jnp.arange(n_blocks, dtype=I32), side='right'), N_EXPERTS - 1).astype(I32)
    n_active = blk_end[-1:].astype(I32)
    return dest, block_e, n_active


def kernel(x, c, ctx, c_ctx, w_mod, b_mod, norm_mix, w_in, b_gate, head_gain, w_pool, pool_scale,
           w_out, norm_ffn, router_w, router_b, w_gate_up, b_gate_up, w_down, b_down, norm_final):
    b, s, d = x.shape
    ctx_len = ctx.shape[1]
    depth = w_mod.shape[0]
    n_lat = b * s
    n_ctx_rows = b * ctx_len
    t_all = n_lat + n_ctx_rows
    d_main = w_in.shape[2] - 4 * N_HEADS
    assert s % ROW_TILE == 0 and n_ctx_rows % ROW_TILE == 0 and ROW_TILE % ctx_len == 0
    assert ROW_TILE % GRID_W == 0 and s % CHUNK == 0 and ctx_len % CHUNK == 0
    assert b + 1 <= MOD_ROWS

    x_all = jnp.concatenate([x.reshape(n_lat, d), ctx.reshape(n_ctx_rows, d)], axis=0)
    c_rows = jnp.zeros((MOD_ROWS, d), F32).at[:b].set(c).at[b].set(c_ctx)
    mods = _modulation(c_rows, w_mod, b_mod).reshape(depth, MOD_ROWS, 1, N_MOD * d)

    tiles_per_batch = s // ROW_TILE
    group_of_tile = lambda i: jnp.minimum(i // tiles_per_batch, b)
    tile_kind = lambda i: (i >= n_lat // ROW_TILE).astype(I32)
    group_of_tile_c = lambda i: jnp.minimum(i // (s // COMBINE_ROWS), b)
    pmats = jnp.stack([_pool_matrices(GRID_W, ROW_TILE), _pool_matrices(ctx_len, ROW_TILE)])

    out = None
    for l in range(depth):
        last = l == depth - 1
        w_main = w_in[l, :, :d_main].astype(BF16)
        w_gate = jnp.pad(w_in[l, :, d_main:], ((0, 0), (0, LANES - 4 * N_HEADS))).astype(BF16)
        bias_row = jnp.pad(b_gate[l].reshape(1, 4 * N_HEADS), ((0, 0), (0, LANES - 4 * N_HEADS)))
        p_main, gates = _in_proj(x_all, mods, l, group_of_tile, norm_mix[l].reshape(1, d), w_main, w_gate)
        hf, hb = _mlstm(p_main, gates, bias_row, b, s, ctx_len)
        rw = jnp.pad(router_w[l], ((0, 0), (0, LANES - N_EXPERTS))).astype(BF16)
        rb = jnp.pad(router_b[l].reshape(1, N_EXPERTS), ((0, 0), (0, LANES - N_EXPERTS)))
        x_mid, h2, logits = _mix_out(
            x_all, p_main, hf, hb, mods, l, group_of_tile, tile_kind,
            head_gain[l].reshape(1, -1), pmats, w_pool[l].astype(BF16), pool_scale[l].reshape(1, -1),
            w_out[l].astype(BF16), norm_ffn[l].reshape(1, d), rw, rb)

        n_tok = n_lat if last else t_all
        idx_pad, gate_pad = _route(logits)
        top_idx = idx_pad[:n_tok, :TOP_K]
        n_assign = n_tok * TOP_K
        n_blocks = -(-n_assign // MOE_ROWS) + N_EXPERTS
        dest, block_e, n_active = _routing_plan(top_idx, n_blocks)
        slots = _dispatch(dest, h2, n_tok, n_blocks * MOE_ROWS)
        y_slots = _moe(slots, block_e, n_active, l, w_gate_up, b_gate_up, w_down, b_down)
        x_next = _combine(dest, gate_pad, x_mid, mods, l, group_of_tile_c, y_slots, n_tok,
                          norm_final.reshape(1, d), last)
        if last:
            out = x_next.reshape(b, s, d)
        else:
            x_all = x_next
    return out
```

```python
import functools

import jax
import jax.numpy as jnp
from jax import lax
from jax.experimental import pallas as pl
from jax.experimental.pallas import tpu as pltpu

F32 = jnp.float32
BF16 = jnp.bfloat16
U32 = jnp.uint32
I32 = jnp.int32

GRID_W = 64
N_HEADS = 4
HEAD_V = 256
HEAD_QK = 128
CHUNK = 128
POOL_WINDOWS = (2, 4, 8, 16)
N_POOL_GROUPS = 4
POOL_GROUP = 256
N_EXPERTS = 32
TOP_K = 4
SWIGLU_ALPHA = 1.702
SWIGLU_LIMIT = 7.0
N_MOD = 6
EPS = 1e-6
LANES = 128
MOD_ROWS = 8

SUBLANES = 8

ROW_TILE = 512
MOE_ROWS = 1024
MOE_FF_TILE = 256
DISPATCH_ROWS = 256
COMBINE_ROWS = 128
VMEM_LIMIT = 56 * 1024 * 1024


def _cparams(sem):
    return pltpu.CompilerParams(dimension_semantics=sem, vmem_limit_bytes=VMEM_LIMIT)


def _pack_bf16_pair(lo, hi):
    lo_bits = pltpu.bitcast(lo.astype(BF16).astype(F32), U32) >> 16
    hi_bits = pltpu.bitcast(hi.astype(BF16).astype(F32), U32) & jnp.uint32(0xFFFF0000)
    return hi_bits | lo_bits


def _unpack_bf16_pair(w):
    lo = pltpu.bitcast(w << 16, F32)
    hi = pltpu.bitcast(w & jnp.uint32(0xFFFF0000), F32)
    return lo, hi


def _tile_chunk(ref, c, n_rows):
    return ref[pl.ds(c, n_rows, stride=SUBLANES), :]


def _store_token_tiles(ref, read_cols, n_rows, d):
    half = d // 2
    assert half == SUBLANES * LANES
    for c in range(SUBLANES):
        lo = read_cols(slice(c * LANES, (c + 1) * LANES))
        hi = read_cols(slice(half + c * LANES, half + (c + 1) * LANES))
        ref[pl.ds(c, n_rows, stride=SUBLANES), :] = _pack_bf16_pair(lo, hi)


def _mod_kernel(c_ref, w_ref, b_ref, o_ref):
    c = c_ref[...]
    s = c * jax.nn.sigmoid(c)
    o_ref[...] = jnp.dot(s, w_ref[...], preferred_element_type=F32,
                         precision=lax.Precision.HIGHEST) + b_ref[...]


def _modulation(c_rows, w_mod, b_mod):
    depth, d, n = w_mod.shape
    tn = 1024
    return pl.pallas_call(
        _mod_kernel,
        grid=(depth, n // tn),
        in_specs=[
            pl.BlockSpec((MOD_ROWS, d), lambda l, j: (0, 0)),
            pl.BlockSpec((None, d, tn), lambda l, j: (l, 0, j)),
            pl.BlockSpec((None, 1, tn), lambda l, j: (l, 0, j)),
        ],
        out_specs=pl.BlockSpec((None, MOD_ROWS, tn), lambda l, j: (l, 0, j)),
        out_shape=jax.ShapeDtypeStruct((depth, MOD_ROWS, n), F32),
        compiler_params=_cparams(("arbitrary", "arbitrary")),
        name="modulation",
    )(c_rows, w_mod, b_mod.reshape(depth, 1, n))


def _in_proj_kernel(x_ref, sh_ref, sc_ref, g_ref, w_ref, wg_ref, p_ref, gate_ref, h_scr):
    @pl.when(pl.program_id(1) == 0)
    def _():
        x = x_ref[...]
        y = x * lax.rsqrt(jnp.mean(x * x, axis=-1, keepdims=True) + EPS) * g_ref[...]
        h = (y * (1.0 + sc_ref[...]) + sh_ref[...]).astype(BF16)
        h_scr[...] = h
        gate_ref[...] = jnp.dot(h, wg_ref[...], preferred_element_type=F32)

    p_ref[...] = jnp.dot(h_scr[...], w_ref[...], preferred_element_type=F32).astype(BF16)


def _in_proj(x_all, mods, layer, group_of_tile, norm_g, w_main, w_gate):
    t_all, d = x_all.shape
    n_main = w_main.shape[1]
    tn = 2048
    mod_spec = lambda k: pl.BlockSpec(
        (None, None, 1, d), lambda i, j: (layer, group_of_tile(i), 0, k))
    return pl.pallas_call(
        _in_proj_kernel,
        grid=(t_all // ROW_TILE, n_main // tn),
        in_specs=[
            pl.BlockSpec((ROW_TILE, d), lambda i, j: (i, 0)),
            mod_spec(0), mod_spec(1),
            pl.BlockSpec((1, d), lambda i, j: (0, 0)),
            pl.BlockSpec((d, tn), lambda i, j: (0, j)),
            pl.BlockSpec((d, LANES), lambda i, j: (0, 0)),
        ],
        out_specs=[
            pl.BlockSpec((ROW_TILE, tn), lambda i, j: (i, j)),
            pl.BlockSpec((ROW_TILE, LANES), lambda i, j: (i, 0)),
        ],
        out_shape=[
            jax.ShapeDtypeStruct((t_all, n_main), BF16),
            jax.ShapeDtypeStruct((t_all, LANES), F32),
        ],
        scratch_shapes=[pltpu.VMEM((ROW_TILE, d), BF16)],
        compiler_params=_cparams(("arbitrary", "arbitrary")),
        name="in_proj",
    )(x_all, mods, mods, norm_g, w_main, w_gate)


def _log_sigmoid(x):
    return jnp.minimum(x, 0.0) - jnp.log(1.0 + jnp.exp(-jnp.abs(x)))


def _mlstm_chain(q, k, v, a, a_t, cs, cs_t, col_i, col_f, reverse, c_ref, n_ref, m_ref):
    bc = cs[:, col_f:col_f + 1]
    br = cs_t[col_f:col_f + 1, :]
    ir = a_t[col_i:col_i + 1, :]
    ic = a[:, col_i:col_i + 1]
    t_idx = lax.broadcasted_iota(I32, (CHUNK, CHUNK), 0)
    s_idx = lax.broadcasted_iota(I32, (CHUNK, CHUNK), 1)
    mask = (s_idx >= t_idx) if reverse else (s_idx <= t_idx)
    dmat = jnp.where(mask, bc - br + ir, -jnp.inf)
    m_prev = m_ref[...]
    m_inter = bc + m_prev
    m_t = jnp.maximum(m_inter, jnp.max(dmat, axis=-1, keepdims=True))
    scale = HEAD_QK ** -0.5
    qk = lax.dot_general(q, k, (((1,), (1,)), ((), ())), preferred_element_type=F32)
    s = qk * (scale * jnp.exp(dmat - m_t))
    w_inter = scale * jnp.exp(m_inter - m_t)
    c_old = c_ref[...]
    num = (jnp.dot(s.astype(BF16), v, preferred_element_type=F32)
           + w_inter * jnp.dot(q, c_old.astype(BF16), preferred_element_type=F32))
    qn = jnp.sum(q.astype(F32) * n_ref[...], axis=-1, keepdims=True)
    den = jnp.sum(s, axis=-1, keepdims=True) + w_inter * qn
    h = num / jnp.maximum(jnp.abs(den), jnp.exp(-m_t))

    b_end = bc[0:1, :] if reverse else bc[CHUNK - 1:CHUNK, :]
    g = b_end - bc + ic
    m_new = jnp.maximum(b_end + m_prev, jnp.max(g, axis=0, keepdims=True))
    wg = jnp.exp(g - m_new)
    wc = jnp.exp(b_end + m_prev - m_new)
    kw = k.astype(F32) * wg
    c_ref[...] = wc * c_old + lax.dot_general(
        kw.astype(BF16), v, (((0,), (0,)), ((), ())), preferred_element_type=F32)
    n_ref[...] = wc * n_ref[...] + jnp.sum(kw, axis=0, keepdims=True)
    m_ref[...] = m_new
    return h


def _mlstm_kernel(qf_ref, kf_ref, vf_ref, gf_ref, qb_ref, kb_ref, vb_ref, gb_ref, bias_ref,
                  hf_ref, hb_ref, c_scr, n_scr, m_scr):
    @pl.when(pl.program_id(1) == 0)
    def _():
        c_scr[...] = jnp.zeros_like(c_scr)
        n_scr[...] = jnp.zeros_like(n_scr)
        m_scr[...] = jnp.zeros_like(m_scr)

    lane = lax.broadcasted_iota(I32, (CHUNK, LANES), 1)
    is_f = ((lane >= N_HEADS) & (lane < 2 * N_HEADS)) | ((lane >= 3 * N_HEADS) & (lane < 4 * N_HEADS))
    t_idx = lax.broadcasted_iota(I32, (CHUNK, CHUNK), 0)
    s_idx = lax.broadcasted_iota(I32, (CHUNK, CHUNK), 1)

    for direction, (q_ref, k_ref, v_ref, g_ref, out_ref) in enumerate(
            ((qf_ref, kf_ref, vf_ref, gf_ref, hf_ref), (qb_ref, kb_ref, vb_ref, gb_ref, hb_ref))):
        reverse = direction == 1
        graw = g_ref[...] + bias_ref[...]
        a = jnp.where(is_f, _log_sigmoid(graw), graw)
        tri = ((s_idx >= t_idx) if reverse else (s_idx <= t_idx)).astype(F32)
        cs = jnp.dot(tri, a, preferred_element_type=F32, precision=lax.Precision.HIGHEST)
        a_t = a.T
        cs_t = cs.T
        for hd in range(N_HEADS):
            col_i = 2 * direction * N_HEADS + hd
            col_f = col_i + N_HEADS
            h = _mlstm_chain(
                q_ref[:, hd * HEAD_QK:(hd + 1) * HEAD_QK],
                k_ref[:, hd * HEAD_QK:(hd + 1) * HEAD_QK],
                v_ref[:, hd * HEAD_V:(hd + 1) * HEAD_V],
                a, a_t, cs, cs_t, col_i, col_f, reverse,
                c_scr.at[direction, hd], n_scr.at[direction, hd], m_scr.at[direction, hd])
            out_ref[:, hd * HEAD_V:(hd + 1) * HEAD_V] = h.astype(BF16)


def _mlstm(p_main, gates, bias_row, batch, seq, ctx_len):
    t_all = p_main.shape[0]
    n_ctx = ctx_len // CHUNK
    n_seq = seq // CHUNK
    ctx_base = batch * seq // CHUNK
    d_qk = N_HEADS * HEAD_QK
    d_v = N_HEADS * HEAD_V

    def fwd_blk(b, i):
        return jnp.where(i < n_ctx, ctx_base + b * n_ctx + i, b * n_seq + (i - n_ctx))

    def bwd_blk(b, i):
        return jnp.where(i < n_ctx, ctx_base + b * n_ctx + (n_ctx - 1 - i),
                         b * n_seq + (n_seq - 1 - (i - n_ctx)))

    def specs(blk):
        return [
            pl.BlockSpec((CHUNK, d_qk), lambda b, i: (blk(b, i), 0)),
            pl.BlockSpec((CHUNK, d_qk), lambda b, i: (blk(b, i), 1)),
            pl.BlockSpec((CHUNK, d_v), lambda b, i: (blk(b, i), 1)),
            pl.BlockSpec((CHUNK, LANES), lambda b, i: (blk(b, i), 0)),
        ]

    return pl.pallas_call(
        _mlstm_kernel,
        grid=(batch, n_ctx + n_seq),
        in_specs=specs(fwd_blk) + specs(bwd_blk) + [pl.BlockSpec((1, LANES), lambda b, i: (0, 0))],
        out_specs=[
            pl.BlockSpec((CHUNK, d_v), lambda b, i: (fwd_blk(b, i), 0)),
            pl.BlockSpec((CHUNK, d_v), lambda b, i: (bwd_blk(b, i), 0)),
        ],
        out_shape=[jax.ShapeDtypeStruct((t_all, d_v), BF16)] * 2,
        scratch_shapes=[
            pltpu.VMEM((2, N_HEADS, HEAD_QK, HEAD_V), F32),
            pltpu.VMEM((2, N_HEADS, 1, HEAD_QK), F32),
            pltpu.VMEM((2, N_HEADS, 1, 1), F32),
        ],
        compiler_params=_cparams(("arbitrary", "arbitrary")),
        name="mlstm",
    )(p_main, p_main, p_main, gates, p_main, p_main, p_main, gates, bias_row)


def _mix_out_kernel(x_ref, o_ref, u_ref, hf_ref, hb_ref, gain_ref, pmat_ref, wpool_ref, pscale_ref,
                    wout_ref, g1_ref, sh2_ref, sc2_ref, nffn_ref, rw_ref, rb_ref,
                    xnew_ref, h2_ref, logit_ref):
    acc = None
    for hd in range(N_HEADS):
        sl = slice(hd * HEAD_V, (hd + 1) * HEAD_V)
        h = hf_ref[:, sl].astype(F32) + hb_ref[:, sl].astype(F32)
        hn = h * lax.rsqrt(jnp.mean(h * h, axis=-1, keepdims=True) + EPS) * gain_ref[:, sl]
        m = (jax.nn.sigmoid(o_ref[:, sl].astype(F32)) * hn).astype(BF16)
        part = jnp.dot(m, wout_ref[sl, :], preferred_element_type=F32)
        acc = part if acc is None else acc + part
    d_mlstm = N_HEADS * HEAD_V
    for g in range(N_POOL_GROUPS):
        sl = slice(g * POOL_GROUP, (g + 1) * POOL_GROUP)
        u = u_ref[:, sl]
        dlt = jnp.dot(pmat_ref[g], u, preferred_element_type=F32) - u.astype(F32)
        y = jnp.dot(dlt.astype(BF16), wpool_ref[g], preferred_element_type=F32) * pscale_ref[:, sl]
        acc = acc + jnp.dot(y.astype(BF16), wout_ref[d_mlstm + g * POOL_GROUP:d_mlstm + (g + 1) * POOL_GROUP, :],
                            preferred_element_type=F32)
    xn = x_ref[...] + g1_ref[...] * acc
    xnew_ref[...] = xn
    y2 = xn * lax.rsqrt(jnp.mean(xn * xn, axis=-1, keepdims=True) + EPS) * nffn_ref[...]
    h2 = y2 * (1.0 + sc2_ref[...]) + sh2_ref[...]
    _store_token_tiles(h2_ref, lambda sl: h2[:, sl], h2.shape[0], h2.shape[1])
    logit_ref[...] = jnp.dot(h2.astype(BF16), rw_ref[...], preferred_element_type=F32) + rb_ref[...]


def _mix_out(x_all, p_main, hf, hb, mods, layer, group_of_tile, tile_kind, gain, pmats, w_pool, pool_scale,
             w_out, norm_ffn, router_w, router_b):
    t_all, d = x_all.shape
    d_half = d // 2
    mod_spec = lambda k: pl.BlockSpec(
        (None, None, 1, d), lambda i: (layer, group_of_tile(i), 0, k))
    const = lambda shape: pl.BlockSpec(shape, lambda i: (0,) * len(shape), pipeline_mode=pl.Buffered(1))
    return pl.pallas_call(
        _mix_out_kernel,
        grid=(t_all // ROW_TILE,),
        in_specs=[
            pl.BlockSpec((ROW_TILE, d), lambda i: (i, 0)),
            pl.BlockSpec((ROW_TILE, d_half), lambda i: (i, 2)),
            pl.BlockSpec((ROW_TILE, d_half), lambda i: (i, 3)),
            pl.BlockSpec((ROW_TILE, d_half), lambda i: (i, 0)),
            pl.BlockSpec((ROW_TILE, d_half), lambda i: (i, 0)),
            const((1, d_half)),
            pl.BlockSpec((None, N_POOL_GROUPS, ROW_TILE, ROW_TILE), lambda i: (tile_kind(i), 0, 0, 0)),
            const((N_POOL_GROUPS, POOL_GROUP, POOL_GROUP)),
            const((1, d_half)),
            const((d, d)),
            mod_spec(2), mod_spec(3), mod_spec(4),
            const((1, d)),
            const((d, LANES)),
            const((1, LANES)),
        ],
        out_specs=[
            pl.BlockSpec((ROW_TILE, d), lambda i: (i, 0)),
            pl.BlockSpec((ROW_TILE * SUBLANES, LANES), lambda i: (i, 0)),
            pl.BlockSpec((ROW_TILE, LANES), lambda i: (i, 0)),
        ],
        out_shape=[
            jax.ShapeDtypeStruct((t_all, d), F32),
            jax.ShapeDtypeStruct((t_all * SUBLANES, LANES), U32),
            jax.ShapeDtypeStruct((t_all, LANES), F32),
        ],
        compiler_params=_cparams(("arbitrary",)),
        name="mix_out",
    )(x_all, p_main, p_main, hf, hb, gain, pmats, w_pool, pool_scale, w_out,
      mods, mods, mods, norm_ffn, router_w, router_b)


def _route_kernel(logit_ref, idx_ref, gate_ref):
    lg = logit_ref[...]
    lane = lax.broadcasted_iota(I32, lg.shape, 1).astype(F32)
    lg = jnp.where(lane < N_EXPERTS, lg, -jnp.inf)
    vals, idxs = [], []
    for _ in range(TOP_K):
        mx = jnp.max(lg, axis=-1, keepdims=True)
        ix = jnp.min(jnp.where(lg == mx, lane, float(LANES)), axis=-1, keepdims=True)
        vals.append(mx)
        idxs.append(ix)
        lg = jnp.where(lane == ix, -jnp.inf, lg)
    es = [jnp.exp(v - vals[0]) for v in vals]
    tot = es[0] + es[1] + es[2] + es[3]
    idx_out = jnp.zeros(lg.shape, F32)
    gate_out = jnp.zeros(lg.shape, F32)
    for k in range(TOP_K):
        idx_out = jnp.where(lane == k, idxs[k], idx_out)
        gate_out = jnp.where(lane == k, es[k] / tot, gate_out)
    idx_ref[...] = idx_out.astype(I32)
    gate_ref[...] = gate_out


def _route(logits):
    t = logits.shape[0]
    spec = pl.BlockSpec((ROW_TILE, LANES), lambda i: (i, 0))
    return pl.pallas_call(
        _route_kernel,
        grid=(t // ROW_TILE,),
        in_specs=[spec],
        out_specs=[spec, spec],
        out_shape=[jax.ShapeDtypeStruct((t, LANES), I32), jax.ShapeDtypeStruct((t, LANES), F32)],
        compiler_params=_cparams(("arbitrary",)),
        name="route",
    )(logits)


def _dispatch_kernel(dest_ref, h_ref, slots_in, slots_hbm, sem):
    del slots_in

    def tile_copy(t, k):
        src = pl.multiple_of(t * SUBLANES, SUBLANES)
        dst = pl.multiple_of(dest_ref[t * TOP_K + k] * SUBLANES, SUBLANES)
        return pltpu.make_async_copy(
            h_ref.at[pl.ds(src, SUBLANES)], slots_hbm.at[pl.ds(dst, SUBLANES)], sem)

    def issue(t, carry):
        for k in range(TOP_K):
            tile_copy(t, k).start(priority=k % 2)
        return carry

    lax.fori_loop(0, DISPATCH_ROWS, issue, 0, unroll=2)

    def drain(t, carry):
        for k in range(TOP_K):
            tile_copy(t, k).wait()
        return carry

    lax.fori_loop(0, DISPATCH_ROWS, drain, 0, unroll=2)


def _dispatch(dest_flat, h_tiles, n_tokens, n_slots):
    slots0 = jnp.zeros((n_slots * SUBLANES, LANES), U32)
    return pl.pallas_call(
        _dispatch_kernel,
        grid=(n_tokens // DISPATCH_ROWS,),
        in_specs=[
            pl.BlockSpec((DISPATCH_ROWS * TOP_K,), lambda i: (i,), memory_space=pltpu.SMEM),
            pl.BlockSpec((DISPATCH_ROWS * SUBLANES, LANES), lambda i: (i, 0)),
            pl.BlockSpec(memory_space=pl.ANY),
        ],
        out_specs=pl.BlockSpec(memory_space=pl.ANY),
        out_shape=jax.ShapeDtypeStruct((n_slots * SUBLANES, LANES), U32),
        scratch_shapes=[pltpu.SemaphoreType.DMA(())],
        input_output_aliases={2: 0},
        compiler_params=_cparams(("arbitrary",)),
        name="dispatch",
    )(dest_flat, h_tiles, slots0)


def _moe_kernel(be_ref, na_ref, x_ref, wg_ref, wu_ref, bg_ref, bu_ref, wd_ref, bd_ref, o_ref,
                x_scr, acc_scr):
    i = pl.program_id(0)
    j = pl.program_id(1)
    n_j = pl.num_programs(1)
    n_rows, d = acc_scr.shape
    half = d // 2

    @pl.when(i < na_ref[0])
    def _():
        @pl.when(j == 0)
        def _():
            for c in range(SUBLANES):
                lo, hi = _unpack_bf16_pair(_tile_chunk(x_ref, c, n_rows))
                x_scr[:, c * LANES:(c + 1) * LANES] = lo.astype(BF16)
                x_scr[:, half + c * LANES:half + (c + 1) * LANES] = hi.astype(BF16)
            acc_scr[...] = jnp.broadcast_to(bd_ref[...], acc_scr.shape)

        x = x_scr[...]
        gl = jnp.dot(x, wg_ref[...].astype(BF16), preferred_element_type=F32) + bg_ref[...]
        ul = jnp.dot(x, wu_ref[...].astype(BF16), preferred_element_type=F32) + bu_ref[...]
        x_glu = jnp.minimum(gl, SWIGLU_LIMIT)
        x_lin = jnp.clip(ul, -SWIGLU_LIMIT, SWIGLU_LIMIT)
        act = x_glu * jax.nn.sigmoid(SWIGLU_ALPHA * x_glu) * (x_lin + 1.0)
        acc_scr[...] += jnp.dot(act.astype(BF16), wd_ref[...].astype(BF16), preferred_element_type=F32)

        @pl.when(j == n_j - 1)
        def _():
            _store_token_tiles(o_ref, lambda sl: acc_scr[:, sl], n_rows, d)


def _moe(slots, block_e, n_active, layer, w_gate_up, b_gate_up, w_down, b_down):
    n_slots = slots.shape[0] // SUBLANES
    d = w_down.shape[3]
    depth, n_e, _, two_f = w_gate_up.shape
    d_ff = two_f // 2
    n_j = d_ff // MOE_FF_TILE
    n_blocks = n_slots // MOE_ROWS

    def blk(i, na):
        return jnp.minimum(i, na[0] - 1)

    def jj(i, j, na):
        return jnp.where(i < na[0], j, n_j - 1)

    grid_spec = pltpu.PrefetchScalarGridSpec(
        num_scalar_prefetch=2,
        grid=(n_blocks, n_j),
        in_specs=[
            pl.BlockSpec((MOE_ROWS * SUBLANES, LANES), lambda i, j, be, na: (blk(i, na), 0)),
            pl.BlockSpec((None, None, d, MOE_FF_TILE),
                         lambda i, j, be, na: (layer, be[blk(i, na)], 0, jj(i, j, na))),
            pl.BlockSpec((None, None, d, MOE_FF_TILE),
                         lambda i, j, be, na: (layer, be[blk(i, na)], 0, n_j + jj(i, j, na))),
            pl.BlockSpec((None, None, 1, MOE_FF_TILE),
                         lambda i, j, be, na: (layer, be[blk(i, na)], 0, jj(i, j, na))),
            pl.BlockSpec((None, None, 1, MOE_FF_TILE),
                         lambda i, j, be, na: (layer, be[blk(i, na)], 0, n_j + jj(i, j, na))),
            pl.BlockSpec((None, None, MOE_FF_TILE, d),
                         lambda i, j, be, na: (layer, be[blk(i, na)], jj(i, j, na), 0)),
            pl.BlockSpec((None, None, 1, d), lambda i, j, be, na: (layer, be[blk(i, na)], 0, 0)),
        ],
        out_specs=pl.BlockSpec((MOE_ROWS * SUBLANES, LANES), lambda i, j, be, na: (blk(i, na), 0)),
        scratch_shapes=[pltpu.VMEM((MOE_ROWS, d), BF16), pltpu.VMEM((MOE_ROWS, d), F32)],
    )
    return pl.pallas_call(
        _moe_kernel,
        grid_spec=grid_spec,
        out_shape=jax.ShapeDtypeStruct((n_slots * SUBLANES, LANES), U32),
        compiler_params=_cparams(("arbitrary", "arbitrary")),
        name="moe",
    )(block_e, n_active, slots, w_gate_up, w_gate_up,
      b_gate_up.reshape(depth, n_e, 1, two_f), b_gate_up.reshape(depth, n_e, 1, two_f),
      w_down, b_down.reshape(depth, n_e, 1, d))


def _combine_kernel(final_norm, dest_ref, gate_ref, x_ref, g2_ref, nf_ref, y_hbm, o_ref, rows, sem):
    def tile_copy(t, k):
        src = pl.multiple_of(dest_ref[t * TOP_K + k] * SUBLANES, SUBLANES)
        dst = pl.multiple_of(t * SUBLANES, SUBLANES)
        return pltpu.make_async_copy(
            y_hbm.at[pl.ds(src, SUBLANES)], rows.at[k, pl.ds(dst, SUBLANES)], sem)

    def issue(t, carry):
        for k in range(TOP_K):
            tile_copy(t, k).start(priority=k % 2)
        return carry

    lax.fori_loop(0, COMBINE_ROWS, issue, 0, unroll=2)

    def drain(t, carry):
        for k in range(TOP_K):
            tile_copy(t, k).wait()
        return carry

    lax.fori_loop(0, COMBINE_ROWS, drain, 0, unroll=2)

    d = x_ref.shape[1]
    half = d // 2
    gates = gate_ref[...]
    gate_cols = [gates[:, k:k + 1] for k in range(TOP_K)]
    sumsq = None
    for c in range(SUBLANES):
        f_lo = None
        f_hi = None
        for k in range(TOP_K):
            lo, hi = _unpack_bf16_pair(_tile_chunk(rows.at[k], c, COMBINE_ROWS))
            f_lo = gate_cols[k] * lo if f_lo is None else f_lo + gate_cols[k] * lo
            f_hi = gate_cols[k] * hi if f_hi is None else f_hi + gate_cols[k] * hi
        sl_lo = slice(c * LANES, (c + 1) * LANES)
        sl_hi = slice(half + c * LANES, half + (c + 1) * LANES)
        y_lo = x_ref[:, sl_lo] + g2_ref[:, sl_lo] * f_lo
        y_hi = x_ref[:, sl_hi] + g2_ref[:, sl_hi] * f_hi
        o_ref[:, sl_lo] = y_lo
        o_ref[:, sl_hi] = y_hi
        if final_norm:
            part = jnp.sum(y_lo * y_lo + y_hi * y_hi, axis=-1, keepdims=True)
            sumsq = part if sumsq is None else sumsq + part
    if final_norm:
        o_ref[...] = o_ref[...] * lax.rsqrt(sumsq / d + EPS) * nf_ref[...]


def _combine(dest_flat, gates, x_all, mods, layer, group_of_tile_c, y_slots, n_tokens, norm_final, final_norm):
    d = x_all.shape[1]
    return pl.pallas_call(
        functools.partial(_combine_kernel, final_norm),
        grid=(n_tokens // COMBINE_ROWS,),
        in_specs=[
            pl.BlockSpec((COMBINE_ROWS * TOP_K,), lambda i: (i,), memory_space=pltpu.SMEM),
            pl.BlockSpec((COMBINE_ROWS, LANES), lambda i: (i, 0)),
            pl.BlockSpec((COMBINE_ROWS, d), lambda i: (i, 0)),
            pl.BlockSpec((None, None, 1, d), lambda i: (layer, group_of_tile_c(i), 0, 5)),
            pl.BlockSpec((1, d), lambda i: (0, 0)),
            pl.BlockSpec(memory_space=pl.ANY),
        ],
        out_specs=pl.BlockSpec((COMBINE_ROWS, d), lambda i: (i, 0)),
        out_shape=jax.ShapeDtypeStruct((n_tokens, d), F32),
        scratch_shapes=[pltpu.VMEM((TOP_K, COMBINE_ROWS * SUBLANES, LANES), U32),
                        pltpu.SemaphoreType.DMA(())],
        compiler_params=_cparams(("arbitrary",)),
        name="combine",
    )(dest_flat, gates, x_all, mods, norm_final, y_slots)


def _pool_matrices(seg_len, tile):
    pos = jnp.arange(tile)
    seg = pos // seg_len
    p = pos % seg_len
    mats = []
    for w in POOL_WINDOWS:
        left = w // 2
        right = w - 1 - left
        lo = jnp.clip(p - left, 0, seg_len)
        hi = jnp.clip(p + right + 1, 0, seg_len)
        inside = ((seg[:, None] == seg[None, :]) & (p[None, :] >= lo[:, None]) & (p[None, :] < hi[:, None]))
        mats.append(jnp.where(inside, 1.0 / (hi - lo).astype(F32)[:, None], 0.0))
    return jnp.stack(mats).astype(BF16)


def _routing_plan(top_idx, n_blocks):
    flat_e = top_idx.reshape(-1)
    onehot = (flat_e[:, None] == jnp.arange(N_EXPERTS, dtype=I32)[None, :]).astype(I32)
    csum = jnp.cumsum(onehot, axis=0)
    rank = jnp.take_along_axis(csum, flat_e[:, None], axis=1)[:, 0] - 1
    counts = csum[-1]
    blocks_e = (counts + MOE_ROWS - 1) // MOE_ROWS
    blk_end = jnp.cumsum(blocks_e)
    start_p = (blk_end - blocks_e) * MOE_ROWS
    dest = (start_p[flat_e] + rank).astype(I32)
    block_e = jnp.minimum(
        jnp.searchsorted(blk_end, jnp.arange(n_blocks, dtype=I32), side='right'), N_EXPERTS - 1).astype(I32)
    n_active = blk_end[-1:].astype(I32)
    return dest, block_e, n_active


def kernel(x, c, ctx, c_ctx, w_mod, b_mod, norm_mix, w_in, b_gate, head_gain, w_pool, pool_scale,
           w_out, norm_ffn, router_w, router_b, w_gate_up, b_gate_up, w_down, b_down, norm_final):
    b, s, d = x.shape
    ctx_len = ctx.shape[1]
    depth = w_mod.shape[0]
    n_lat = b * s
    n_ctx_rows = b * ctx_len
    t_all = n_lat + n_ctx_rows
    d_main = w_in.shape[2] - 4 * N_HEADS
    assert s % ROW_TILE == 0 and n_ctx_rows % ROW_TILE == 0 and ROW_TILE % ctx_len == 0
    assert ROW_TILE % GRID_W == 0 and s % CHUNK == 0 and ctx_len % CHUNK == 0
    assert b + 1 <= MOD_ROWS

    x_all = jnp.concatenate([x.reshape(n_lat, d), ctx.reshape(n_ctx_rows, d)], axis=0)
    c_rows = jnp.zeros((MOD_ROWS, d), F32).at[:b].set(c).at[b].set(c_ctx)
    mods = _modulation(c_rows, w_mod, b_mod).reshape(depth, MOD_ROWS, 1, N_MOD * d)

    tiles_per_batch = s // ROW_TILE
    group_of_tile = lambda i: jnp.minimum(i // tiles_per_batch, b)
    tile_kind = lambda i: (i >= n_lat // ROW_TILE).astype(I32)
    group_of_tile_c = lambda i: jnp.minimum(i // (s // COMBINE_ROWS), b)
    pmats = jnp.stack([_pool_matrices(GRID_W, ROW_TILE), _pool_matrices(ctx_len, ROW_TILE)])

    out = None
    for l in range(depth):
        last = l == depth - 1
        w_main = w_in[l, :, :d_main].astype(BF16)
        w_gate = jnp.pad(w_in[l, :, d_main:], ((0, 0), (0, LANES - 4 * N_HEADS))).astype(BF16)
        bias_row = jnp.pad(b_gate[l].reshape(1, 4 * N_HEADS), ((0, 0), (0, LANES - 4 * N_HEADS)))
        p_main, gates = _in_proj(x_all, mods, l, group_of_tile, norm_mix[l].reshape(1, d), w_main, w_gate)
        hf, hb = _mlstm(p_main, gates, bias_row, b, s, ctx_len)
        rw = jnp.pad(router_w[l], ((0, 0), (0, LANES - N_EXPERTS))).astype(BF16)
        rb = jnp.pad(router_b[l].reshape(1, N_EXPERTS), ((0, 0), (0, LANES - N_EXPERTS)))
        x_mid, h2, logits = _mix_out(
            x_all, p_main, hf, hb, mods, l, group_of_tile, tile_kind,
            head_gain[l].reshape(1, -1), pmats, w_pool[l].astype(BF16), pool_scale[l].reshape(1, -1),
            w_out[l].astype(BF16), norm_ffn[l].reshape(1, d), rw, rb)

        n_tok = n_lat if last else t_all
        idx_pad, gate_pad = _route(logits)
        top_idx = idx_pad[:n_tok, :TOP_K]
        n_assign = n_tok * TOP_K
        n_blocks = -(-n_assign // MOE_ROWS) + N_EXPERTS
        dest, block_e, n_active = _routing_plan(top_idx, n_blocks)
        slots = _dispatch(dest, h2, n_tok, n_blocks * MOE_ROWS)
        y_slots = _moe(slots, block_e, n_active, l, w_gate_up, b_gate_up, w_down, b_down)
        x_next = _combine(dest, gate_pad, x_mid, mods, l, group_of_tile_c, y_slots, n_tok,
                          norm_final.reshape(1, d), last)
        if last:
            out = x_next.reshape(b, s, d)
        else:
            x_all = x_next
    return out
```

```python
import functools

import jax
import jax.numpy as jnp
from jax import lax
from jax.experimental import pallas as pl
from jax.experimental.pallas import tpu as pltpu

F32 = jnp.float32
BF16 = jnp.bfloat16
U32 = jnp.uint32
I32 = jnp.int32

GRID_W = 64
N_HEADS = 4
HEAD_V = 256
HEAD_QK = 128
CHUNK = 128
POOL_WINDOWS = (2, 4, 8, 16)
N_POOL_GROUPS = 4
POOL_GROUP = 256
N_EXPERTS = 32
TOP_K = 4
SWIGLU_ALPHA = 1.702
SWIGLU_LIMIT = 7.0
N_MOD = 6
EPS = 1e-6
LANES = 128
MOD_ROWS = 8

SUBLANES = 8

ROW_TILE = 512
MOE_ROWS = 1024
MOE_FF_TILE = 256
DISPATCH_ROWS = 256
COMBINE_ROWS = 128
VMEM_LIMIT = 56 * 1024 * 1024


def _cparams(sem):
    return pltpu.CompilerParams(dimension_semantics=sem, vmem_limit_bytes=VMEM_LIMIT)


def _pack_bf16_pair(lo, hi):
    lo_bits = pltpu.bitcast(lo.astype(BF16).astype(F32), U32) >> 16
    hi_bits = pltpu.bitcast(hi.astype(BF16).astype(F32), U32) & jnp.uint32(0xFFFF0000)
    return hi_bits | lo_bits


def _unpack_bf16_pair(w):
    lo = pltpu.bitcast(w << 16, F32)
    hi = pltpu.bitcast(w & jnp.uint32(0xFFFF0000), F32)
    return lo, hi


def _tile_chunk(ref, c, n_rows):
    return ref[pl.ds(c, n_rows, stride=SUBLANES), :]


def _store_token_tiles(ref, read_cols, n_rows, d):
    half = d // 2
    assert half == SUBLANES * LANES
    for c in range(SUBLANES):
        lo = read_cols(slice(c * LANES, (c + 1) * LANES))
        hi = read_cols(slice(half + c * LANES, half + (c + 1) * LANES))
        ref[pl.ds(c, n_rows, stride=SUBLANES), :] = _pack_bf16_pair(lo, hi)


def _mod_kernel(c_ref, w_ref, b_ref, o_ref):
    c = c_ref[...]
    s = c * jax.nn.sigmoid(c)
    o_ref[...] = jnp.dot(s, w_ref[...], preferred_element_type=F32,
                         precision=lax.Precision.HIGHEST) + b_ref[...]


def _modulation(c_rows, w_mod, b_mod):
    depth, d, n = w_mod.shape
    tn = 1024
    return pl.pallas_call(
        _mod_kernel,
        grid=(depth, n // tn),
        in_specs=[
            pl.BlockSpec((MOD_ROWS, d), lambda l, j: (0, 0)),
            pl.BlockSpec((None, d, tn), lambda l, j: (l, 0, j)),
            pl.BlockSpec((None, 1, tn), lambda l, j: (l, 0, j)),
        ],
        out_specs=pl.BlockSpec((None, MOD_ROWS, tn), lambda l, j: (l, 0, j)),
        out_shape=jax.ShapeDtypeStruct((depth, MOD_ROWS, n), F32),
        compiler_params=_cparams(("arbitrary", "arbitrary")),
        name="modulation",
    )(c_rows, w_mod, b_mod.reshape(depth, 1, n))


def _in_proj_kernel(x_ref, sh_ref, sc_ref, g_ref, w_ref, wg_ref, p_ref, gate_ref, h_scr):
    @pl.when(pl.program_id(1) == 0)
    def _():
        x = x_ref[...]
        y = x * lax.rsqrt(jnp.mean(x * x, axis=-1, keepdims=True) + EPS) * g_ref[...]
        h = (y * (1.0 + sc_ref[...]) + sh_ref[...]).astype(BF16)
        h_scr[...] = h
        gate_ref[...] = jnp.dot(h, wg_ref[...], preferred_element_type=F32)

    p_ref[...] = jnp.dot(h_scr[...], w_ref[...], preferred_element_type=F32).astype(BF16)


def _in_proj(x_all, mods, layer, group_of_tile, norm_g, w_main, w_gate):
    t_all, d = x_all.shape
    n_main = w_main.shape[1]
    tn = 2048
    mod_spec = lambda k: pl.BlockSpec(
        (None, None, 1, d), lambda i, j: (layer, group_of_tile(i), 0, k))
    return pl.pallas_call(
        _in_proj_kernel,
        grid=(t_all // ROW_TILE, n_main // tn),
        in_specs=[
            pl.BlockSpec((ROW_TILE, d), lambda i, j: (i, 0)),
            mod_spec(0), mod_spec(1),
            pl.BlockSpec((1, d), lambda i, j: (0, 0)),
            pl.BlockSpec((d, tn), lambda i, j: (0, j)),
            pl.BlockSpec((d, LANES), lambda i, j: (0, 0)),
        ],
        out_specs=[
            pl.BlockSpec((ROW_TILE, tn), lambda i, j: (i, j)),
            pl.BlockSpec((ROW_TILE, LANES), lambda i, j: (i, 0)),
        ],
        out_shape=[
            jax.ShapeDtypeStruct((t_all, n_main), BF16),
            jax.ShapeDtypeStruct((t_all, LANES), F32),
        ],
        scratch_shapes=[pltpu.VMEM((ROW_TILE, d), BF16)],
        compiler_params=_cparams(("arbitrary", "arbitrary")),
        name="in_proj",
    )(x_all, mods, mods, norm_g, w_main, w_gate)


def _log_sigmoid(x):
    return jnp.minimum(x, 0.0) - jnp.log(1.0 + jnp.exp(-jnp.abs(x)))


def _mlstm_chain(q, k, v, a, a_t, cs, cs_t, col_i, col_f, reverse, c_ref, n_ref, m_ref):
    bc = cs[:, col_f:col_f + 1]
    br = cs_t[col_f:col_f + 1, :]
    ir = a_t[col_i:col_i + 1, :]
    ic = a[:, col_i:col_i + 1]
    t_idx = lax.broadcasted_iota(I32, (CHUNK, CHUNK), 0)
    s_idx = lax.broadcasted_iota(I32, (CHUNK, CHUNK), 1)
    mask = (s_idx >= t_idx) if reverse else (s_idx <= t_idx)
    dmat = jnp.where(mask, bc - br + ir, -jnp.inf)
    m_prev = m_ref[...]
    m_inter = bc + m_prev
    m_t = jnp.maximum(m_inter, jnp.max(dmat, axis=-1, keepdims=True))
    scale = HEAD_QK ** -0.5
    qk = lax.dot_general(q, k, (((1,), (1,)), ((), ())), preferred_element_type=F32)
    s = qk * (scale * jnp.exp(dmat - m_t))
    w_inter = scale * jnp.exp(m_inter - m_t)
    c_old = c_ref[...]
    num = (jnp.dot(s.astype(BF16), v, preferred_element_type=F32)
           + w_inter * jnp.dot(q, c_old.astype(BF16), preferred_element_type=F32))
    qn = jnp.sum(q.astype(F32) * n_ref[...], axis=-1, keepdims=True)
    den = jnp.sum(s, axis=-1, keepdims=True) + w_inter * qn
    h = num / jnp.maximum(jnp.abs(den), jnp.exp(-m_t))

    b_end = bc[0:1, :] if reverse else bc[CHUNK - 1:CHUNK, :]
    g = b_end - bc + ic
    m_new = jnp.maximum(b_end + m_prev, jnp.max(g, axis=0, keepdims=True))
    wg = jnp.exp(g - m_new)
    wc = jnp.exp(b_end + m_prev - m_new)
    kw = k.astype(F32) * wg
    c_ref[...] = wc * c_old + lax.dot_general(
        kw.astype(BF16), v, (((0,), (0,)), ((), ())), preferred_element_type=F32)
    n_ref[...] = wc * n_ref[...] + jnp.sum(kw, axis=0, keepdims=True)
    m_ref[...] = m_new
    return h


def _mlstm_kernel(qf_ref, kf_ref, vf_ref, gf_ref, qb_ref, kb_ref, vb_ref, gb_ref, bias_ref,
                  hf_ref, hb_ref, c_scr, n_scr, m_scr):
    @pl.when(pl.program_id(1) == 0)
    def _():
        c_scr[...] = jnp.zeros_like(c_scr)
        n_scr[...] = jnp.zeros_like(n_scr)
        m_scr[...] = jnp.zeros_like(m_scr)

    lane = lax.broadcasted_iota(I32, (CHUNK, LANES), 1)
    is_f = ((lane >= N_HEADS) & (lane < 2 * N_HEADS)) | ((lane >= 3 * N_HEADS) & (lane < 4 * N_HEADS))
    t_idx = lax.broadcasted_iota(I32, (CHUNK, CHUNK), 0)
    s_idx = lax.broadcasted_iota(I32, (CHUNK, CHUNK), 1)

    for direction, (q_ref, k_ref, v_ref, g_ref, out_ref) in enumerate(
            ((qf_ref, kf_ref, vf_ref, gf_ref, hf_ref), (qb_ref, kb_ref, vb_ref, gb_ref, hb_ref))):
        reverse = direction == 1
        graw = g_ref[...] + bias_ref[...]
        a = jnp.where(is_f, _log_sigmoid(graw), graw)
        tri = ((s_idx >= t_idx) if reverse else (s_idx <= t_idx)).astype(F32)
        cs = jnp.dot(tri, a, preferred_element_type=F32, precision=lax.Precision.HIGHEST)
        a_t = a.T
        cs_t = cs.T
        for hd in range(N_HEADS):
            col_i = 2 * direction * N_HEADS + hd
            col_f = col_i + N_HEADS
            h = _mlstm_chain(
                q_ref[:, hd * HEAD_QK:(hd + 1) * HEAD_QK],
                k_ref[:, hd * HEAD_QK:(hd + 1) * HEAD_QK],
                v_ref[:, hd * HEAD_V:(hd + 1) * HEAD_V],
                a, a_t, cs, cs_t, col_i, col_f, reverse,
                c_scr.at[direction, hd], n_scr.at[direction, hd], m_scr.at[direction, hd])
            out_ref[:, hd * HEAD_V:(hd + 1) * HEAD_V] = h.astype(BF16)


def _mlstm(p_main, gates, bias_row, batch, seq, ctx_len):
    t_all = p_main.shape[0]
    n_ctx = ctx_len // CHUNK
    n_seq = seq // CHUNK
    ctx_base = batch * seq // CHUNK
    d_qk = N_HEADS * HEAD_QK
    d_v = N_HEADS * HEAD_V

    def fwd_blk(b, i):
        return jnp.where(i < n_ctx, ctx_base + b * n_ctx + i, b * n_seq + (i - n_ctx))

    def bwd_blk(b, i):
        return jnp.where(i < n_ctx, ctx_base + b * n_ctx + (n_ctx - 1 - i),
                         b * n_seq + (n_seq - 1 - (i - n_ctx)))

    def specs(blk):
        return [
            pl.BlockSpec((CHUNK, d_qk), lambda b, i: (blk(b, i), 0)),
            pl.BlockSpec((CHUNK, d_qk), lambda b, i: (blk(b, i), 1)),
            pl.BlockSpec((CHUNK, d_v), lambda b, i: (blk(b, i), 1)),
            pl.BlockSpec((CHUNK, LANES), lambda b, i: (blk(b, i), 0)),
        ]

    return pl.pallas_call(
        _mlstm_kernel,
        grid=(batch, n_ctx + n_seq),
        in_specs=specs(fwd_blk) + specs(bwd_blk) + [pl.BlockSpec((1, LANES), lambda b, i: (0, 0))],
        out_specs=[
            pl.BlockSpec((CHUNK, d_v), lambda b, i: (fwd_blk(b, i), 0)),
            pl.BlockSpec((CHUNK, d_v), lambda b, i: (bwd_blk(b, i), 0)),
        ],
        out_shape=[jax.ShapeDtypeStruct((t_all, d_v), BF16)] * 2,
        scratch_shapes=[
            pltpu.VMEM((2, N_HEADS, HEAD_QK, HEAD_V), F32),
            pltpu.VMEM((2, N_HEADS, 1, HEAD_QK), F32),
            pltpu.VMEM((2, N_HEADS, 1, 1), F32),
        ],
        compiler_params=_cparams(("arbitrary", "arbitrary")),
        name="mlstm",
    )(p_main, p_main, p_main, gates, p_main, p_main, p_main, gates, bias_row)


def _mix_out_kernel(x_ref, o_ref, u_ref, hf_ref, hb_ref, gain_ref, pmat_ref, wpool_ref, pscale_ref,
                    wout_ref, g1_ref, sh2_ref, sc2_ref, nffn_ref, rw_ref, rb_ref,
                    xnew_ref, h2_ref, logit_ref, mix_scr):
    for hd in range(N_HEADS):
        sl = slice(hd * HEAD_V, (hd + 1) * HEAD_V)
        h = hf_ref[:, sl].astype(F32) + hb_ref[:, sl].astype(F32)
        hn = h * lax.rsqrt(jnp.mean(h * h, axis=-1, keepdims=True) + EPS) * gain_ref[:, sl]
        mix_scr[:, sl] = (jax.nn.sigmoid(o_ref[:, sl].astype(F32)) * hn).astype(BF16)
    d_mlstm = N_HEADS * HEAD_V
    for g in range(N_POOL_GROUPS):
        sl = slice(g * POOL_GROUP, (g + 1) * POOL_GROUP)
        u = u_ref[:, sl]
        dlt = jnp.dot(pmat_ref[g], u, preferred_element_type=F32) - u.astype(F32)
        y = jnp.dot(dlt.astype(BF16), wpool_ref[g], preferred_element_type=F32) * pscale_ref[:, sl]
        mix_scr[:, d_mlstm + g * POOL_GROUP:d_mlstm + (g + 1) * POOL_GROUP] = y.astype(BF16)
    acc = jnp.dot(mix_scr[...], wout_ref[...], preferred_element_type=F32)
    xn = x_ref[...] + g1_ref[...] * acc
    xnew_ref[...] = xn
    y2 = xn * lax.rsqrt(jnp.mean(xn * xn, axis=-1, keepdims=True) + EPS) * nffn_ref[...]
    h2 = y2 * (1.0 + sc2_ref[...]) + sh2_ref[...]
    _store_token_tiles(h2_ref, lambda sl: h2[:, sl], h2.shape[0], h2.shape[1])
    logit_ref[...] = jnp.dot(h2.astype(BF16), rw_ref[...], preferred_element_type=F32) + rb_ref[...]


def _mix_out(x_all, p_main, hf, hb, mods, layer, group_of_tile, tile_kind, gain, pmats, w_pool, pool_scale,
             w_out, norm_ffn, router_w, router_b):
    t_all, d = x_all.shape
    d_half = d // 2
    mod_spec = lambda k: pl.BlockSpec(
        (None, None, 1, d), lambda i: (layer, group_of_tile(i), 0, k))
    const = lambda shape: pl.BlockSpec(shape, lambda i: (0,) * len(shape), pipeline_mode=pl.Buffered(1))
    return pl.pallas_call(
        _mix_out_kernel,
        grid=(t_all // ROW_TILE,),
        in_specs=[
            pl.BlockSpec((ROW_TILE, d), lambda i: (i, 0)),
            pl.BlockSpec((ROW_TILE, d_half), lambda i: (i, 2)),
            pl.BlockSpec((ROW_TILE, d_half), lambda i: (i, 3)),
            pl.BlockSpec((ROW_TILE, d_half), lambda i: (i, 0)),
            pl.BlockSpec((ROW_TILE, d_half), lambda i: (i, 0)),
            const((1, d_half)),
            pl.BlockSpec((None, N_POOL_GROUPS, ROW_TILE, ROW_TILE), lambda i: (tile_kind(i), 0, 0, 0)),
            const((N_POOL_GROUPS, POOL_GROUP, POOL_GROUP)),
            const((1, d_half)),
            const((d, d)),
            mod_spec(2), mod_spec(3), mod_spec(4),
            const((1, d)),
            const((d, LANES)),
            const((1, LANES)),
        ],
        out_specs=[
            pl.BlockSpec((ROW_TILE, d), lambda i: (i, 0)),
            pl.BlockSpec((ROW_TILE * SUBLANES, LANES), lambda i: (i, 0)),
            pl.BlockSpec((ROW_TILE, LANES), lambda i: (i, 0)),
        ],
        out_shape=[
            jax.ShapeDtypeStruct((t_all, d), F32),
            jax.ShapeDtypeStruct((t_all * SUBLANES, LANES), U32),
            jax.ShapeDtypeStruct((t_all, LANES), F32),
        ],
        scratch_shapes=[pltpu.VMEM((ROW_TILE, d), BF16)],
        compiler_params=_cparams(("arbitrary",)),
        name="mix_out",
    )(x_all, p_main, p_main, hf, hb, gain, pmats, w_pool, pool_scale, w_out,
      mods, mods, mods, norm_ffn, router_w, router_b)


def _route_kernel(logit_ref, idx_ref, gate_ref, count_ref, run_scr):
    @pl.when(pl.program_id(0) == 0)
    def _():
        run_scr[...] = jnp.zeros_like(run_scr)

    lg = logit_ref[...]
    n_rows = lg.shape[0]
    lane = lax.broadcasted_iota(I32, lg.shape, 1).astype(F32)
    lg = jnp.where(lane < N_EXPERTS, lg, -jnp.inf)
    vals, idxs, onehots = [], [], []
    for _ in range(TOP_K):
        mx = jnp.max(lg, axis=-1, keepdims=True)
        ix = jnp.min(jnp.where(lg == mx, lane, float(LANES)), axis=-1, keepdims=True)
        hit = lane == ix
        vals.append(mx)
        idxs.append(ix)
        onehots.append(hit.astype(F32))
        lg = jnp.where(hit, -jnp.inf, lg)
    es = [jnp.exp(v - vals[0]) for v in vals]
    tot = es[0] + es[1] + es[2] + es[3]

    onehot = onehots[0] + onehots[1] + onehots[2] + onehots[3]
    r_idx = lax.broadcasted_iota(I32, (n_rows, n_rows), 0)
    c_idx = lax.broadcasted_iota(I32, (n_rows, n_rows), 1)
    earlier = (c_idx < r_idx).astype(BF16)
    before = jnp.dot(earlier, onehot.astype(BF16), preferred_element_type=F32) + run_scr[...]
    run_scr[...] += jnp.sum(onehot, axis=0, keepdims=True)

    idx_out = jnp.zeros(lg.shape, F32)
    gate_out = jnp.zeros(lg.shape, F32)
    for k in range(TOP_K):
        rank = jnp.sum(onehots[k] * before, axis=-1, keepdims=True)
        idx_out = jnp.where(lane == k, idxs[k], idx_out)
        idx_out = jnp.where(lane == TOP_K + k, rank, idx_out)
        gate_out = jnp.where(lane == k, es[k] / tot, gate_out)
    idx_ref[...] = idx_out.astype(I32)
    gate_ref[...] = gate_out
    count_ref[...] = jnp.broadcast_to(run_scr[...], count_ref.shape).astype(I32)


def _route(logits, n_tokens):
    spec = pl.BlockSpec((ROW_TILE, LANES), lambda i: (i, 0))
    return pl.pallas_call(
        _route_kernel,
        grid=(n_tokens // ROW_TILE,),
        in_specs=[spec],
        out_specs=[spec, spec, pl.BlockSpec((SUBLANES, LANES), lambda i: (0, 0))],
        out_shape=[jax.ShapeDtypeStruct((n_tokens, LANES), I32), jax.ShapeDtypeStruct((n_tokens, LANES), F32),
                   jax.ShapeDtypeStruct((SUBLANES, LANES), I32)],
        scratch_shapes=[pltpu.VMEM((1, LANES), F32)],
        compiler_params=_cparams(("arbitrary",)),
        name="route",
    )(logits)


def _dispatch_kernel(dest_ref, h_ref, slots_hbm, sem):
    def tile_copy(t, k):
        src = pl.multiple_of(t * SUBLANES, SUBLANES)
        dst = pl.multiple_of(dest_ref[t * TOP_K + k] * SUBLANES, SUBLANES)
        return pltpu.make_async_copy(
            h_ref.at[pl.ds(src, SUBLANES)], slots_hbm.at[pl.ds(dst, SUBLANES)], sem)

    def issue(t, carry):
        for k in range(TOP_K):
            tile_copy(t, k).start(priority=k % 2)
        return carry

    lax.fori_loop(0, DISPATCH_ROWS, issue, 0, unroll=2)

    def drain(t, carry):
        for k in range(TOP_K):
            tile_copy(t, k).wait()
        return carry

    lax.fori_loop(0, DISPATCH_ROWS, drain, 0, unroll=2)


def _dispatch(dest_flat, h_tiles, n_tokens, n_slots):
    return pl.pallas_call(
        _dispatch_kernel,
        grid=(n_tokens // DISPATCH_ROWS,),
        in_specs=[
            pl.BlockSpec((DISPATCH_ROWS * TOP_K,), lambda i: (i,), memory_space=pltpu.SMEM),
            pl.BlockSpec((DISPATCH_ROWS * SUBLANES, LANES), lambda i: (i, 0)),
        ],
        out_specs=pl.BlockSpec(memory_space=pl.ANY),
        out_shape=jax.ShapeDtypeStruct((n_slots * SUBLANES, LANES), U32),
        scratch_shapes=[pltpu.SemaphoreType.DMA(())],
        compiler_params=_cparams(("arbitrary",)),
        name="dispatch",
    )(dest_flat, h_tiles)


def _moe_kernel(be_ref, nv_ref, na_ref, x_ref, wg_ref, wu_ref, bg_ref, bu_ref, wd_ref, bd_ref, o_ref,
                x_scr, acc_scr):
    i = pl.program_id(0)
    j = pl.program_id(1)
    n_j = pl.num_programs(1)
    n_rows, d = acc_scr.shape
    half = d // 2

    @pl.when(i < na_ref[0])
    def _():
        @pl.when(j == 0)
        def _():
            live = lax.broadcasted_iota(I32, (n_rows, LANES), 0) < nv_ref[i]
            for c in range(SUBLANES):
                words = jnp.where(live, _tile_chunk(x_ref, c, n_rows), jnp.uint32(0))
                lo, hi = _unpack_bf16_pair(words)
                x_scr[:, c * LANES:(c + 1) * LANES] = lo.astype(BF16)
                x_scr[:, half + c * LANES:half + (c + 1) * LANES] = hi.astype(BF16)
            acc_scr[...] = jnp.broadcast_to(bd_ref[...], acc_scr.shape)

        def expert_mlp(m):
            x = x_scr[:m, :]
            gl = jnp.dot(x, wg_ref[...].astype(BF16), preferred_element_type=F32) + bg_ref[...]
            ul = jnp.dot(x, wu_ref[...].astype(BF16), preferred_element_type=F32) + bu_ref[...]
            x_glu = jnp.minimum(gl, SWIGLU_LIMIT)
            x_lin = jnp.clip(ul, -SWIGLU_LIMIT, SWIGLU_LIMIT)
            act = x_glu * jax.nn.sigmoid(SWIGLU_ALPHA * x_glu) * (x_lin + 1.0)
            acc_scr[:m, :] += jnp.dot(act.astype(BF16), wd_ref[...].astype(BF16), preferred_element_type=F32)

        @pl.when(nv_ref[i] > n_rows // 2)
        def _():
            expert_mlp(n_rows)

        @pl.when(nv_ref[i] <= n_rows // 2)
        def _():
            expert_mlp(n_rows // 2)

        @pl.when(j == n_j - 1)
        def _():
            _store_token_tiles(o_ref, lambda sl: acc_scr[:, sl], n_rows, d)


def _moe(slots, block_e, block_rows, n_active, layer, w_gate_up, b_gate_up, w_down, b_down):
    n_slots = slots.shape[0] // SUBLANES
    d = w_down.shape[3]
    depth, n_e, _, two_f = w_gate_up.shape
    d_ff = two_f // 2
    n_j = d_ff // MOE_FF_TILE
    n_blocks = n_slots // MOE_ROWS

    def blk(i, na):
        return jnp.minimum(i, na[0] - 1)

    def jj(i, j, na):
        return jnp.where(i < na[0], j, n_j - 1)

    grid_spec = pltpu.PrefetchScalarGridSpec(
        num_scalar_prefetch=3,
        grid=(n_blocks, n_j),
        in_specs=[
            pl.BlockSpec((MOE_ROWS * SUBLANES, LANES), lambda i, j, be, nv, na: (blk(i, na), 0)),
            pl.BlockSpec((None, None, d, MOE_FF_TILE),
                         lambda i, j, be, nv, na: (layer, be[blk(i, na)], 0, jj(i, j, na))),
            pl.BlockSpec((None, None, d, MOE_FF_TILE),
                         lambda i, j, be, nv, na: (layer, be[blk(i, na)], 0, n_j + jj(i, j, na))),
            pl.BlockSpec((None, None, 1, MOE_FF_TILE),
                         lambda i, j, be, nv, na: (layer, be[blk(i, na)], 0, jj(i, j, na))),
            pl.BlockSpec((None, None, 1, MOE_FF_TILE),
                         lambda i, j, be, nv, na: (layer, be[blk(i, na)], 0, n_j + jj(i, j, na))),
            pl.BlockSpec((None, None, MOE_FF_TILE, d),
                         lambda i, j, be, nv, na: (layer, be[blk(i, na)], jj(i, j, na), 0)),
            pl.BlockSpec((None, None, 1, d), lambda i, j, be, nv, na: (layer, be[blk(i, na)], 0, 0)),
        ],
        out_specs=pl.BlockSpec((MOE_ROWS * SUBLANES, LANES), lambda i, j, be, nv, na: (blk(i, na), 0)),
        scratch_shapes=[pltpu.VMEM((MOE_ROWS, d), BF16), pltpu.VMEM((MOE_ROWS, d), F32)],
    )
    return pl.pallas_call(
        _moe_kernel,
        grid_spec=grid_spec,
        out_shape=jax.ShapeDtypeStruct((n_slots * SUBLANES, LANES), U32),
        compiler_params=_cparams(("arbitrary", "arbitrary")),
        name="moe",
    )(block_e, block_rows, n_active, slots, w_gate_up, w_gate_up,
      b_gate_up.reshape(depth, n_e, 1, two_f), b_gate_up.reshape(depth, n_e, 1, two_f),
      w_down, b_down.reshape(depth, n_e, 1, d))


def _combine_kernel(final_norm, dest_ref, dest_next_ref, gate_ref, x_ref, g2_ref, nf_ref, y_hbm, o_ref,
                    rows_scr, sems):
    i = pl.program_id(0)
    slot = i % 2

    def tile_copy(d_ref, buf, t, k):
        src = pl.multiple_of(d_ref[t * TOP_K + k] * SUBLANES, SUBLANES)
        dst = pl.multiple_of(t * SUBLANES, SUBLANES)
        return pltpu.make_async_copy(
            y_hbm.at[pl.ds(src, SUBLANES)], rows_scr.at[buf, k, pl.ds(dst, SUBLANES)], sems.at[buf])

    def start_tile(d_ref, buf):
        def issue(t, carry):
            for k in range(TOP_K):
                tile_copy(d_ref, buf, t, k).start(priority=k % 2)
            return carry
        lax.fori_loop(0, COMBINE_ROWS, issue, 0, unroll=2)

    @pl.when(i == 0)
    def _():
        start_tile(dest_ref, 0)

    @pl.when(i + 1 < pl.num_programs(0))
    def _():
        start_tile(dest_next_ref, 1 - slot)

    def drain(t, carry):
        for k in range(TOP_K):
            tile_copy(dest_ref, slot, t, k).wait()
        return carry

    lax.fori_loop(0, COMBINE_ROWS, drain, 0, unroll=2)

    rows = rows_scr.at[slot]
    d = x_ref.shape[1]
    half = d // 2
    gates = gate_ref[...]
    gate_cols = [gates[:, k:k + 1] for k in range(TOP_K)]
    sumsq = None
    for c in range(SUBLANES):
        f_lo = None
        f_hi = None
        for k in range(TOP_K):
            lo, hi = _unpack_bf16_pair(_tile_chunk(rows.at[k], c, COMBINE_ROWS))
            f_lo = gate_cols[k] * lo if f_lo is None else f_lo + gate_cols[k] * lo
            f_hi = gate_cols[k] * hi if f_hi is None else f_hi + gate_cols[k] * hi
        sl_lo = slice(c * LANES, (c + 1) * LANES)
        sl_hi = slice(half + c * LANES, half + (c + 1) * LANES)
        y_lo = x_ref[:, sl_lo] + g2_ref[:, sl_lo] * f_lo
        y_hi = x_ref[:, sl_hi] + g2_ref[:, sl_hi] * f_hi
        o_ref[:, sl_lo] = y_lo
        o_ref[:, sl_hi] = y_hi
        if final_norm:
            part = jnp.sum(y_lo * y_lo + y_hi * y_hi, axis=-1, keepdims=True)
            sumsq = part if sumsq is None else sumsq + part
    if final_norm:
        o_ref[...] = o_ref[...] * lax.rsqrt(sumsq / d + EPS) * nf_ref[...]


def _combine(dest_flat, gates, x_all, mods, layer, group_of_tile_c, y_slots, n_tokens, norm_final, final_norm):
    d = x_all.shape[1]
    n_steps = n_tokens // COMBINE_ROWS
    return pl.pallas_call(
        functools.partial(_combine_kernel, final_norm),
        grid=(n_steps,),
        in_specs=[
            pl.BlockSpec((COMBINE_ROWS * TOP_K,), lambda i: (i,), memory_space=pltpu.SMEM),
            pl.BlockSpec((COMBINE_ROWS * TOP_K,), lambda i: (jnp.minimum(i + 1, n_steps - 1),),
                         memory_space=pltpu.SMEM),
            pl.BlockSpec((COMBINE_ROWS, LANES), lambda i: (i, 0)),
            pl.BlockSpec((COMBINE_ROWS, d), lambda i: (i, 0)),
            pl.BlockSpec((None, None, 1, d), lambda i: (layer, group_of_tile_c(i), 0, 5)),
            pl.BlockSpec((1, d), lambda i: (0, 0)),
            pl.BlockSpec(memory_space=pl.ANY),
        ],
        out_specs=pl.BlockSpec((COMBINE_ROWS, d), lambda i: (i, 0)),
        out_shape=jax.ShapeDtypeStruct((n_tokens, d), F32),
        scratch_shapes=[pltpu.VMEM((2, TOP_K, COMBINE_ROWS * SUBLANES, LANES), U32),
                        pltpu.SemaphoreType.DMA((2,))],
        compiler_params=_cparams(("arbitrary",)),
        name="combine",
    )(dest_flat, dest_flat, gates, x_all, mods, norm_final, y_slots)


def _pool_matrices(seg_len, tile):
    pos = jnp.arange(tile)
    seg = pos // seg_len
    p = pos % seg_len
    mats = []
    for w in POOL_WINDOWS:
        left = w // 2
        right = w - 1 - left
        lo = jnp.clip(p - left, 0, seg_len)
        hi = jnp.clip(p + right + 1, 0, seg_len)
        inside = ((seg[:, None] == seg[None, :]) & (p[None, :] >= lo[:, None]) & (p[None, :] < hi[:, None]))
        mats.append(jnp.where(inside, 1.0 / (hi - lo).astype(F32)[:, None], 0.0))
    return jnp.stack(mats).astype(BF16)


def _routing_plan(top_idx, rank, counts, n_blocks):
    experts = jnp.arange(N_EXPERTS, dtype=I32)
    blocks_e = (counts + MOE_ROWS - 1) // MOE_ROWS
    blk_end = jnp.cumsum(blocks_e)
    blk_start = blk_end - blocks_e
    start_e = blk_start * MOE_ROWS
    start_of = jnp.sum(jnp.where(top_idx[..., None] == experts, start_e, 0), axis=-1)
    dest = (start_of + rank).reshape(-1).astype(I32)
    blk = jnp.arange(n_blocks, dtype=I32)
    block_e = jnp.minimum(jnp.sum((blk[:, None] >= blk_end[None, :]).astype(I32), axis=-1), N_EXPERTS - 1)
    in_e = block_e[:, None] == experts
    local = blk - jnp.sum(jnp.where(in_e, blk_start, 0), axis=-1)
    rows_e = jnp.sum(jnp.where(in_e, counts, 0), axis=-1)
    block_rows = jnp.clip(rows_e - local * MOE_ROWS, 0, MOE_ROWS).astype(I32)
    n_active = blk_end[-1:].astype(I32)
    return dest, block_e.astype(I32), block_rows, n_active


def kernel(x, c, ctx, c_ctx, w_mod, b_mod, norm_mix, w_in, b_gate, head_gain, w_pool, pool_scale,
           w_out, norm_ffn, router_w, router_b, w_gate_up, b_gate_up, w_down, b_down, norm_final):
    b, s, d = x.shape
    ctx_len = ctx.shape[1]
    depth = w_mod.shape[0]
    n_lat = b * s
    n_ctx_rows = b * ctx_len
    t_all = n_lat + n_ctx_rows
    d_main = w_in.shape[2] - 4 * N_HEADS
    assert s % ROW_TILE == 0 and n_ctx_rows % ROW_TILE == 0 and ROW_TILE % ctx_len == 0
    assert ROW_TILE % GRID_W == 0 and s % CHUNK == 0 and ctx_len % CHUNK == 0
    assert b + 1 <= MOD_ROWS

    x_all = jnp.concatenate([x.reshape(n_lat, d), ctx.reshape(n_ctx_rows, d)], axis=0)
    c_rows = jnp.zeros((MOD_ROWS, d), F32).at[:b].set(c).at[b].set(c_ctx)
    mods = _modulation(c_rows, w_mod, b_mod).reshape(depth, MOD_ROWS, 1, N_MOD * d)

    tiles_per_batch = s // ROW_TILE
    group_of_tile = lambda i: jnp.minimum(i // tiles_per_batch, b)
    tile_kind = lambda i: (i >= n_lat // ROW_TILE).astype(I32)
    group_of_tile_c = lambda i: jnp.minimum(i // (s // COMBINE_ROWS), b)
    pmats = jnp.stack([_pool_matrices(GRID_W, ROW_TILE), _pool_matrices(ctx_len, ROW_TILE)])

    out = None
    for l in range(depth):
        last = l == depth - 1
        w_main = w_in[l, :, :d_main].astype(BF16)
        w_gate = jnp.pad(w_in[l, :, d_main:], ((0, 0), (0, LANES - 4 * N_HEADS))).astype(BF16)
        bias_row = jnp.pad(b_gate[l].reshape(1, 4 * N_HEADS), ((0, 0), (0, LANES - 4 * N_HEADS)))
        p_main, gates = _in_proj(x_all, mods, l, group_of_tile, norm_mix[l].reshape(1, d), w_main, w_gate)
        hf, hb = _mlstm(p_main, gates, bias_row, b, s, ctx_len)
        rw = jnp.pad(router_w[l], ((0, 0), (0, LANES - N_EXPERTS))).astype(BF16)
        rb = jnp.pad(router_b[l].reshape(1, N_EXPERTS), ((0, 0), (0, LANES - N_EXPERTS)))
        x_mid, h2, logits = _mix_out(
            x_all, p_main, hf, hb, mods, l, group_of_tile, tile_kind,
            head_gain[l].reshape(1, -1), pmats, w_pool[l].astype(BF16), pool_scale[l].reshape(1, -1),
            w_out[l].astype(BF16), norm_ffn[l].reshape(1, d), rw, rb)

        n_tok = n_lat if last else t_all
        idx_pad, gate_pad, count_pad = _route(logits, n_tok)
        n_assign = n_tok * TOP_K
        n_blocks = -(-n_assign // MOE_ROWS) + N_EXPERTS
        dest, block_e, block_rows, n_active = _routing_plan(
            idx_pad[:, :TOP_K], idx_pad[:, TOP_K:2 * TOP_K], count_pad[0, :N_EXPERTS], n_blocks)
        slots = _dispatch(dest, h2, n_tok, n_blocks * MOE_ROWS)
        y_slots = _moe(slots, block_e, block_rows, n_active, l, w_gate_up, b_gate_up, w_down, b_down)
        x_next = _combine(dest, gate_pad, x_mid, mods, l, group_of_tile_c, y_slots, n_tok,
                          norm_final.reshape(1, d), last)
        if last:
            out = x_next.reshape(b, s, d)
        else:
            x_all = x_next
    return out
```

```python
import functools

import jax
import jax.numpy as jnp
from jax import lax
from jax.experimental import pallas as pl
from jax.experimental.pallas import tpu as pltpu

F32 = jnp.float32
BF16 = jnp.bfloat16
U32 = jnp.uint32
I32 = jnp.int32

GRID_W = 64
N_HEADS = 4
HEAD_V = 256
HEAD_QK = 128
CHUNK = 128
POOL_WINDOWS = (2, 4, 8, 16)
N_POOL_GROUPS = 4
POOL_GROUP = 256
N_EXPERTS = 32
TOP_K = 4
SWIGLU_ALPHA = 1.702
SWIGLU_LIMIT = 7.0
N_MOD = 6
EPS = 1e-6
LANES = 128
MOD_ROWS = 8

SUBLANES = 8

ROW_TILE = 512
MOE_ROWS = 1024
MOE_FF_TILE = 512
DISPATCH_ROWS = 256
COMBINE_ROWS = 128
VMEM_LIMIT = 56 * 1024 * 1024
MOE_VMEM_LIMIT = 60 * 1024 * 1024


def _cparams(sem, vmem_limit=VMEM_LIMIT):
    return pltpu.CompilerParams(dimension_semantics=sem, vmem_limit_bytes=vmem_limit)


def _pack_bf16_pair(lo, hi):
    lo_bits = pltpu.bitcast(lo.astype(BF16).astype(F32), U32) >> 16
    hi_bits = pltpu.bitcast(hi.astype(BF16).astype(F32), U32) & jnp.uint32(0xFFFF0000)
    return hi_bits | lo_bits


def _unpack_bf16_pair(w):
    lo = pltpu.bitcast(w << 16, F32)
    hi = pltpu.bitcast(w & jnp.uint32(0xFFFF0000), F32)
    return lo, hi


def _tile_chunk(ref, c, n_rows):
    return ref[pl.ds(c, n_rows, stride=SUBLANES), :]


def _store_token_tiles(ref, read_cols, n_rows, d):
    half = d // 2
    assert half == SUBLANES * LANES
    for c in range(SUBLANES):
        lo = read_cols(slice(c * LANES, (c + 1) * LANES))
        hi = read_cols(slice(half + c * LANES, half + (c + 1) * LANES))
        ref[pl.ds(c, n_rows, stride=SUBLANES), :] = _pack_bf16_pair(lo, hi)


def _mod_kernel(c_ref, w_ref, b_ref, o_ref):
    c = c_ref[...]
    s = c * jax.nn.sigmoid(c)
    o_ref[...] = jnp.dot(s, w_ref[...], preferred_element_type=F32,
                         precision=lax.Precision.HIGHEST) + b_ref[...]


def _modulation(c_rows, w_mod, b_mod):
    depth, d, n = w_mod.shape
    tn = 1024
    return pl.pallas_call(
        _mod_kernel,
        grid=(depth, n // tn),
        in_specs=[
            pl.BlockSpec((MOD_ROWS, d), lambda l, j: (0, 0)),
            pl.BlockSpec((None, d, tn), lambda l, j: (l, 0, j)),
            pl.BlockSpec((None, 1, tn), lambda l, j: (l, 0, j)),
        ],
        out_specs=pl.BlockSpec((None, MOD_ROWS, tn), lambda l, j: (l, 0, j)),
        out_shape=jax.ShapeDtypeStruct((depth, MOD_ROWS, n), F32),
        compiler_params=_cparams(("arbitrary", "arbitrary")),
        name="modulation",
    )(c_rows, w_mod, b_mod.reshape(depth, 1, n))


def _in_proj_kernel(x_ref, sh_ref, sc_ref, g_ref, w_ref, wg_ref, p_ref, gate_ref, h_scr):
    @pl.when(pl.program_id(1) == 0)
    def _():
        x = x_ref[...]
        y = x * lax.rsqrt(jnp.mean(x * x, axis=-1, keepdims=True) + EPS) * g_ref[...]
        h = (y * (1.0 + sc_ref[...]) + sh_ref[...]).astype(BF16)
        h_scr[...] = h
        gate_ref[...] = jnp.dot(h, wg_ref[...], preferred_element_type=F32)

    p_ref[...] = jnp.dot(h_scr[...], w_ref[...], preferred_element_type=F32).astype(BF16)


def _in_proj(x_all, mods, layer, group_of_tile, norm_g, w_main, w_gate):
    t_all, d = x_all.shape
    n_main = w_main.shape[1]
    tn = 2048
    mod_spec = lambda k: pl.BlockSpec(
        (None, None, 1, d), lambda i, j: (layer, group_of_tile(i), 0, k))
    return pl.pallas_call(
        _in_proj_kernel,
        grid=(t_all // ROW_TILE, n_main // tn),
        in_specs=[
            pl.BlockSpec((ROW_TILE, d), lambda i, j: (i, 0)),
            mod_spec(0), mod_spec(1),
            pl.BlockSpec((1, d), lambda i, j: (0, 0)),
            pl.BlockSpec((d, tn), lambda i, j: (0, j)),
            pl.BlockSpec((d, LANES), lambda i, j: (0, 0)),
        ],
        out_specs=[
            pl.BlockSpec((ROW_TILE, tn), lambda i, j: (i, j)),
            pl.BlockSpec((ROW_TILE, LANES), lambda i, j: (i, 0)),
        ],
        out_shape=[
            jax.ShapeDtypeStruct((t_all, n_main), BF16),
            jax.ShapeDtypeStruct((t_all, LANES), F32),
        ],
        scratch_shapes=[pltpu.VMEM((ROW_TILE, d), BF16)],
        compiler_params=_cparams(("arbitrary", "arbitrary")),
        name="in_proj",
    )(x_all, mods, mods, norm_g, w_main, w_gate)


def _log_sigmoid(x):
    return jnp.minimum(x, 0.0) - jnp.log(1.0 + jnp.exp(-jnp.abs(x)))


def _mlstm_chain(q, k, v, a, a_t, cs, cs_t, col_i, col_f, reverse, c_ref, n_ref, m_ref):
    bc = cs[:, col_f:col_f + 1]
    br = cs_t[col_f:col_f + 1, :]
    ir = a_t[col_i:col_i + 1, :]
    ic = a[:, col_i:col_i + 1]
    t_idx = lax.broadcasted_iota(I32, (CHUNK, CHUNK), 0)
    s_idx = lax.broadcasted_iota(I32, (CHUNK, CHUNK), 1)
    mask = (s_idx >= t_idx) if reverse else (s_idx <= t_idx)
    dmat = jnp.where(mask, bc - br + ir, -jnp.inf)
    m_prev = m_ref[...]
    m_inter = bc + m_prev
    m_t = jnp.maximum(m_inter, jnp.max(dmat, axis=-1, keepdims=True))
    scale = HEAD_QK ** -0.5
    qk = lax.dot_general(q, k, (((1,), (1,)), ((), ())), preferred_element_type=F32)
    s = qk * (scale * jnp.exp(dmat - m_t))
    w_inter = scale * jnp.exp(m_inter - m_t)
    c_old = c_ref[...]
    num = (jnp.dot(s.astype(BF16), v, preferred_element_type=F32)
           + w_inter * jnp.dot(q, c_old.astype(BF16), preferred_element_type=F32))
    qn = jnp.sum(q.astype(F32) * n_ref[...], axis=-1, keepdims=True)
    den = jnp.sum(s, axis=-1, keepdims=True) + w_inter * qn
    h = num / jnp.maximum(jnp.abs(den), jnp.exp(-m_t))

    b_end = bc[0:1, :] if reverse else bc[CHUNK - 1:CHUNK, :]
    g = b_end - bc + ic
    m_new = jnp.maximum(b_end + m_prev, jnp.max(g, axis=0, keepdims=True))
    wg = jnp.exp(g - m_new)
    wc = jnp.exp(b_end + m_prev - m_new)
    kw = k.astype(F32) * wg
    c_ref[...] = wc * c_old + lax.dot_general(
        kw.astype(BF16), v, (((0,), (0,)), ((), ())), preferred_element_type=F32)
    n_ref[...] = wc * n_ref[...] + jnp.sum(kw, axis=0, keepdims=True)
    m_ref[...] = m_new
    return h


def _mlstm_kernel(qf_ref, kf_ref, vf_ref, gf_ref, qb_ref, kb_ref, vb_ref, gb_ref, bias_ref,
                  hf_ref, hb_ref, c_scr, n_scr, m_scr):
    @pl.when(pl.program_id(1) == 0)
    def _():
        c_scr[...] = jnp.zeros_like(c_scr)
        n_scr[...] = jnp.zeros_like(n_scr)
        m_scr[...] = jnp.zeros_like(m_scr)

    lane = lax.broadcasted_iota(I32, (CHUNK, LANES), 1)
    is_f = ((lane >= N_HEADS) & (lane < 2 * N_HEADS)) | ((lane >= 3 * N_HEADS) & (lane < 4 * N_HEADS))
    t_idx = lax.broadcasted_iota(I32, (CHUNK, CHUNK), 0)
    s_idx = lax.broadcasted_iota(I32, (CHUNK, CHUNK), 1)

    for direction, (q_ref, k_ref, v_ref, g_ref, out_ref) in enumerate(
            ((qf_ref, kf_ref, vf_ref, gf_ref, hf_ref), (qb_ref, kb_ref, vb_ref, gb_ref, hb_ref))):
        reverse = direction == 1
        graw = g_ref[...] + bias_ref[...]
        a = jnp.where(is_f, _log_sigmoid(graw), graw)
        tri = ((s_idx >= t_idx) if reverse else (s_idx <= t_idx)).astype(F32)
        cs = jnp.dot(tri, a, preferred_element_type=F32, precision=lax.Precision.HIGHEST)
        a_t = a.T
        cs_t = cs.T
        for hd in range(N_HEADS):
            col_i = 2 * direction * N_HEADS + hd
            col_f = col_i + N_HEADS
            h = _mlstm_chain(
                q_ref[:, hd * HEAD_QK:(hd + 1) * HEAD_QK],
                k_ref[:, hd * HEAD_QK:(hd + 1) * HEAD_QK],
                v_ref[:, hd * HEAD_V:(hd + 1) * HEAD_V],
                a, a_t, cs, cs_t, col_i, col_f, reverse,
                c_scr.at[direction, hd], n_scr.at[direction, hd], m_scr.at[direction, hd])
            out_ref[:, hd * HEAD_V:(hd + 1) * HEAD_V] = h.astype(BF16)


def _mlstm(p_main, gates, bias_row, batch, seq, ctx_len):
    t_all = p_main.shape[0]
    n_ctx = ctx_len // CHUNK
    n_seq = seq // CHUNK
    ctx_base = batch * seq // CHUNK
    d_qk = N_HEADS * HEAD_QK
    d_v = N_HEADS * HEAD_V

    def fwd_blk(b, i):
        return jnp.where(i < n_ctx, ctx_base + b * n_ctx + i, b * n_seq + (i - n_ctx))

    def bwd_blk(b, i):
        return jnp.where(i < n_ctx, ctx_base + b * n_ctx + (n_ctx - 1 - i),
                         b * n_seq + (n_seq - 1 - (i - n_ctx)))

    def specs(blk):
        return [
            pl.BlockSpec((CHUNK, d_qk), lambda b, i: (blk(b, i), 0)),
            pl.BlockSpec((CHUNK, d_qk), lambda b, i: (blk(b, i), 1)),
            pl.BlockSpec((CHUNK, d_v), lambda b, i: (blk(b, i), 1)),
            pl.BlockSpec((CHUNK, LANES), lambda b, i: (blk(b, i), 0)),
        ]

    return pl.pallas_call(
        _mlstm_kernel,
        grid=(batch, n_ctx + n_seq),
        in_specs=specs(fwd_blk) + specs(bwd_blk) + [pl.BlockSpec((1, LANES), lambda b, i: (0, 0))],
        out_specs=[
            pl.BlockSpec((CHUNK, d_v), lambda b, i: (fwd_blk(b, i), 0)),
            pl.BlockSpec((CHUNK, d_v), lambda b, i: (bwd_blk(b, i), 0)),
        ],
        out_shape=[jax.ShapeDtypeStruct((t_all, d_v), BF16)] * 2,
        scratch_shapes=[
            pltpu.VMEM((2, N_HEADS, HEAD_QK, HEAD_V), F32),
            pltpu.VMEM((2, N_HEADS, 1, HEAD_QK), F32),
            pltpu.VMEM((2, N_HEADS, 1, 1), F32),
        ],
        compiler_params=_cparams(("arbitrary", "arbitrary")),
        name="mlstm",
    )(p_main, p_main, p_main, gates, p_main, p_main, p_main, gates, bias_row)


def _mix_out_kernel(x_ref, o_ref, u_ref, hf_ref, hb_ref, gain_ref, pmat_ref, wpool_ref, pscale_ref,
                    wout_ref, g1_ref, sh2_ref, sc2_ref, nffn_ref, rw_ref, rb_ref,
                    xnew_ref, h2_ref, logit_ref, mix_scr):
    for hd in range(N_HEADS):
        sl = slice(hd * HEAD_V, (hd + 1) * HEAD_V)
        h = hf_ref[:, sl].astype(F32) + hb_ref[:, sl].astype(F32)
        hn = h * lax.rsqrt(jnp.mean(h * h, axis=-1, keepdims=True) + EPS) * gain_ref[:, sl]
        mix_scr[:, sl] = (jax.nn.sigmoid(o_ref[:, sl].astype(F32)) * hn).astype(BF16)
    d_mlstm = N_HEADS * HEAD_V
    for g in range(N_POOL_GROUPS):
        sl = slice(g * POOL_GROUP, (g + 1) * POOL_GROUP)
        u = u_ref[:, sl]
        dlt = jnp.dot(pmat_ref[g], u, preferred_element_type=F32) - u.astype(F32)
        y = jnp.dot(dlt.astype(BF16), wpool_ref[g], preferred_element_type=F32) * pscale_ref[:, sl]
        mix_scr[:, d_mlstm + g * POOL_GROUP:d_mlstm + (g + 1) * POOL_GROUP] = y.astype(BF16)
    acc = jnp.dot(mix_scr[...], wout_ref[...], preferred_element_type=F32)
    xn = x_ref[...] + g1_ref[...] * acc
    xnew_ref[...] = xn
    y2 = xn * lax.rsqrt(jnp.mean(xn * xn, axis=-1, keepdims=True) + EPS) * nffn_ref[...]
    h2 = y2 * (1.0 + sc2_ref[...]) + sh2_ref[...]
    _store_token_tiles(h2_ref, lambda sl: h2[:, sl], h2.shape[0], h2.shape[1])
    logit_ref[...] = jnp.dot(h2.astype(BF16), rw_ref[...], preferred_element_type=F32) + rb_ref[...]


def _mix_out(x_all, p_main, hf, hb, mods, layer, group_of_tile, tile_kind, gain, pmats, w_pool, pool_scale,
             w_out, norm_ffn, router_w, router_b):
    t_all, d = x_all.shape
    d_half = d // 2
    mod_spec = lambda k: pl.BlockSpec(
        (None, None, 1, d), lambda i: (layer, group_of_tile(i), 0, k))
    const = lambda shape: pl.BlockSpec(shape, lambda i: (0,) * len(shape), pipeline_mode=pl.Buffered(1))
    return pl.pallas_call(
        _mix_out_kernel,
        grid=(t_all // ROW_TILE,),
        in_specs=[
            pl.BlockSpec((ROW_TILE, d), lambda i: (i, 0)),
            pl.BlockSpec((ROW_TILE, d_half), lambda i: (i, 2)),
            pl.BlockSpec((ROW_TILE, d_half), lambda i: (i, 3)),
            pl.BlockSpec((ROW_TILE, d_half), lambda i: (i, 0)),
            pl.BlockSpec((ROW_TILE, d_half), lambda i: (i, 0)),
            const((1, d_half)),
            pl.BlockSpec((None, N_POOL_GROUPS, ROW_TILE, ROW_TILE), lambda i: (tile_kind(i), 0, 0, 0)),
            const((N_POOL_GROUPS, POOL_GROUP, POOL_GROUP)),
            const((1, d_half)),
            const((d, d)),
            mod_spec(2), mod_spec(3), mod_spec(4),
            const((1, d)),
            const((d, LANES)),
            const((1, LANES)),
        ],
        out_specs=[
            pl.BlockSpec((ROW_TILE, d), lambda i: (i, 0)),
            pl.BlockSpec((ROW_TILE * SUBLANES, LANES), lambda i: (i, 0)),
            pl.BlockSpec((ROW_TILE, LANES), lambda i: (i, 0)),
        ],
        out_shape=[
            jax.ShapeDtypeStruct((t_all, d), F32),
            jax.ShapeDtypeStruct((t_all * SUBLANES, LANES), U32),
            jax.ShapeDtypeStruct((t_all, LANES), F32),
        ],
        scratch_shapes=[pltpu.VMEM((ROW_TILE, d), BF16)],
        compiler_params=_cparams(("arbitrary",)),
        name="mix_out",
    )(x_all, p_main, p_main, hf, hb, gain, pmats, w_pool, pool_scale, w_out,
      mods, mods, mods, norm_ffn, router_w, router_b)


def _route_kernel(logit_ref, idx_ref, gate_ref, count_ref, run_scr):
    @pl.when(pl.program_id(0) == 0)
    def _():
        run_scr[...] = jnp.zeros_like(run_scr)

    lg = logit_ref[...]
    n_rows = lg.shape[0]
    lane = lax.broadcasted_iota(I32, lg.shape, 1).astype(F32)
    lg = jnp.where(lane < N_EXPERTS, lg, -jnp.inf)
    vals, idxs, onehots = [], [], []
    for _ in range(TOP_K):
        mx = jnp.max(lg, axis=-1, keepdims=True)
        ix = jnp.min(jnp.where(lg == mx, lane, float(LANES)), axis=-1, keepdims=True)
        hit = lane == ix
        vals.append(mx)
        idxs.append(ix)
        onehots.append(hit.astype(F32))
        lg = jnp.where(hit, -jnp.inf, lg)
    es = [jnp.exp(v - vals[0]) for v in vals]
    tot = es[0] + es[1] + es[2] + es[3]

    onehot = onehots[0] + onehots[1] + onehots[2] + onehots[3]
    r_idx = lax.broadcasted_iota(I32, (n_rows, n_rows), 0)
    c_idx = lax.broadcasted_iota(I32, (n_rows, n_rows), 1)
    earlier = (c_idx < r_idx).astype(BF16)
    before = jnp.dot(earlier, onehot.astype(BF16), preferred_element_type=F32) + run_scr[...]
    run_scr[...] += jnp.sum(onehot, axis=0, keepdims=True)

    idx_out = jnp.zeros(lg.shape, F32)
    gate_out = jnp.zeros(lg.shape, F32)
    for k in range(TOP_K):
        rank = jnp.sum(onehots[k] * before, axis=-1, keepdims=True)
        idx_out = jnp.where(lane == k, idxs[k], idx_out)
        idx_out = jnp.where(lane == TOP_K + k, rank, idx_out)
        gate_out = jnp.where(lane == k, es[k] / tot, gate_out)
    idx_ref[...] = idx_out.astype(I32)
    gate_ref[...] = gate_out
    count_ref[...] = jnp.broadcast_to(run_scr[...], count_ref.shape).astype(I32)


def _route(logits, n_tokens):
    spec = pl.BlockSpec((ROW_TILE, LANES), lambda i: (i, 0))
    return pl.pallas_call(
        _route_kernel,
        grid=(n_tokens // ROW_TILE,),
        in_specs=[spec],
        out_specs=[spec, spec, pl.BlockSpec((SUBLANES, LANES), lambda i: (0, 0))],
        out_shape=[jax.ShapeDtypeStruct((n_tokens, LANES), I32), jax.ShapeDtypeStruct((n_tokens, LANES), F32),
                   jax.ShapeDtypeStruct((SUBLANES, LANES), I32)],
        scratch_shapes=[pltpu.VMEM((1, LANES), F32)],
        compiler_params=_cparams(("arbitrary",)),
        name="route",
    )(logits)


def _dispatch_kernel(dest_ref, h_ref, slots_hbm, sem):
    def tile_copy(t, k):
        src = pl.multiple_of(t * SUBLANES, SUBLANES)
        dst = pl.multiple_of(dest_ref[t * TOP_K + k] * SUBLANES, SUBLANES)
        return pltpu.make_async_copy(
            h_ref.at[pl.ds(src, SUBLANES)], slots_hbm.at[pl.ds(dst, SUBLANES)], sem)

    def issue(t, carry):
        for k in range(TOP_K):
            tile_copy(t, k).start(priority=k % 2)
        return carry

    lax.fori_loop(0, DISPATCH_ROWS, issue, 0, unroll=2)

    def drain(t, carry):
        for k in range(TOP_K):
            tile_copy(t, k).wait()
        return carry

    lax.fori_loop(0, DISPATCH_ROWS, drain, 0, unroll=2)


def _dispatch(dest_flat, h_tiles, n_tokens, n_slots):
    return pl.pallas_call(
        _dispatch_kernel,
        grid=(n_tokens // DISPATCH_ROWS,),
        in_specs=[
            pl.BlockSpec((DISPATCH_ROWS * TOP_K,), lambda i: (i,), memory_space=pltpu.SMEM),
            pl.BlockSpec((DISPATCH_ROWS * SUBLANES, LANES), lambda i: (i, 0)),
        ],
        out_specs=pl.BlockSpec(memory_space=pl.ANY),
        out_shape=jax.ShapeDtypeStruct((n_slots * SUBLANES, LANES), U32),
        scratch_shapes=[pltpu.SemaphoreType.DMA(())],
        compiler_params=_cparams(("arbitrary",)),
        name="dispatch",
    )(dest_flat, h_tiles)


def _moe_kernel(be_ref, nv_ref, na_ref, x_ref, wg_ref, wu_ref, bg_ref, bu_ref, wd_ref, bd_ref, o_ref,
                x_scr, acc_scr):
    i = pl.program_id(0)
    j = pl.program_id(1)
    n_j = pl.num_programs(1)
    n_rows, d = acc_scr.shape
    half = d // 2

    @pl.when(i < na_ref[0])
    def _():
        @pl.when(j == 0)
        def _():
            live = lax.broadcasted_iota(I32, (n_rows, LANES), 0) < nv_ref[i]
            for c in range(SUBLANES):
                words = jnp.where(live, _tile_chunk(x_ref, c, n_rows), jnp.uint32(0))
                lo, hi = _unpack_bf16_pair(words)
                x_scr[:, c * LANES:(c + 1) * LANES] = lo.astype(BF16)
                x_scr[:, half + c * LANES:half + (c + 1) * LANES] = hi.astype(BF16)

            @pl.when(i == 0)
            def _():
                acc_scr[...] = jnp.zeros_like(acc_scr)

        def expert_mlp(m):
            x = x_scr[:m, :]
            gl = jnp.dot(x, wg_ref[...].astype(BF16), preferred_element_type=F32) + bg_ref[...]
            ul = jnp.dot(x, wu_ref[...].astype(BF16), preferred_element_type=F32) + bu_ref[...]
            x_glu = jnp.minimum(gl, SWIGLU_LIMIT)
            x_lin = jnp.clip(ul, -SWIGLU_LIMIT, SWIGLU_LIMIT)
            act = x_glu * jax.nn.sigmoid(SWIGLU_ALPHA * x_glu) * (x_lin + 1.0)
            part = jnp.dot(act.astype(BF16), wd_ref[...].astype(BF16), preferred_element_type=F32)
            base = jnp.where(j == 0, jnp.broadcast_to(bd_ref[...], (m, d)), acc_scr[:m, :])
            acc_scr[:m, :] = base + part

        @pl.when(nv_ref[i] > n_rows // 2)
        def _():
            expert_mlp(n_rows)

        @pl.when(nv_ref[i] <= n_rows // 2)
        def _():
            expert_mlp(n_rows // 2)

        @pl.when(j == n_j - 1)
        def _():
            _store_token_tiles(o_ref, lambda sl: acc_scr[:, sl], n_rows, d)


def _moe(slots, block_e, block_rows, n_active, layer, w_gate_up, b_gate_up, w_down, b_down):
    n_slots = slots.shape[0] // SUBLANES
    d = w_down.shape[3]
    depth, n_e, _, two_f = w_gate_up.shape
    d_ff = two_f // 2
    n_j = d_ff // MOE_FF_TILE
    n_blocks = n_slots // MOE_ROWS

    def blk(i, na):
        return jnp.minimum(i, na[0] - 1)

    def jj(i, j, na):
        return jnp.where(i < na[0], j, n_j - 1)

    grid_spec = pltpu.PrefetchScalarGridSpec(
        num_scalar_prefetch=3,
        grid=(n_blocks, n_j),
        in_specs=[
            pl.BlockSpec((MOE_ROWS * SUBLANES, LANES), lambda i, j, be, nv, na: (blk(i, na), 0)),
            pl.BlockSpec((None, None, d, MOE_FF_TILE),
                         lambda i, j, be, nv, na: (layer, be[blk(i, na)], 0, jj(i, j, na))),
            pl.BlockSpec((None, None, d, MOE_FF_TILE),
                         lambda i, j, be, nv, na: (layer, be[blk(i, na)], 0, n_j + jj(i, j, na))),
            pl.BlockSpec((None, None, 1, MOE_FF_TILE),
                         lambda i, j, be, nv, na: (layer, be[blk(i, na)], 0, jj(i, j, na))),
            pl.BlockSpec((None, None, 1, MOE_FF_TILE),
                         lambda i, j, be, nv, na: (layer, be[blk(i, na)], 0, n_j + jj(i, j, na))),
            pl.BlockSpec((None, None, MOE_FF_TILE, d),
                         lambda i, j, be, nv, na: (layer, be[blk(i, na)], jj(i, j, na), 0)),
            pl.BlockSpec((None, None, 1, d), lambda i, j, be, nv, na: (layer, be[blk(i, na)], 0, 0)),
        ],
        out_specs=pl.BlockSpec((MOE_ROWS * SUBLANES, LANES), lambda i, j, be, nv, na: (blk(i, na), 0)),
        scratch_shapes=[pltpu.VMEM((MOE_ROWS, d), BF16), pltpu.VMEM((MOE_ROWS, d), F32)],
    )
    return pl.pallas_call(
        _moe_kernel,
        grid_spec=grid_spec,
        out_shape=jax.ShapeDtypeStruct((n_slots * SUBLANES, LANES), U32),
        compiler_params=_cparams(("arbitrary", "arbitrary"), MOE_VMEM_LIMIT),
        name="moe",
    )(block_e, block_rows, n_active, slots, w_gate_up, w_gate_up,
      b_gate_up.reshape(depth, n_e, 1, two_f), b_gate_up.reshape(depth, n_e, 1, two_f),
      w_down, b_down.reshape(depth, n_e, 1, d))


def _combine_kernel(final_norm, dest_ref, dest_next_ref, gate_ref, x_ref, g2_ref, nf_ref, y_hbm, o_ref,
                    rows_scr, sems):
    i = pl.program_id(0)
    slot = i % 2

    def tile_copy(d_ref, buf, t, k):
        src = pl.multiple_of(d_ref[t * TOP_K + k] * SUBLANES, SUBLANES)
        dst = pl.multiple_of(t * SUBLANES, SUBLANES)
        return pltpu.make_async_copy(
            y_hbm.at[pl.ds(src, SUBLANES)], rows_scr.at[buf, k, pl.ds(dst, SUBLANES)], sems.at[buf])

    def start_tile(d_ref, buf):
        def issue(t, carry):
            for k in range(TOP_K):
                tile_copy(d_ref, buf, t, k).start(priority=k % 2)
            return carry
        lax.fori_loop(0, COMBINE_ROWS, issue, 0, unroll=2)

    @pl.when(i == 0)
    def _():
        start_tile(dest_ref, 0)

    @pl.when(i + 1 < pl.num_programs(0))
    def _():
        start_tile(dest_next_ref, 1 - slot)

    def drain(t, carry):
        for k in range(TOP_K):
            tile_copy(dest_ref, slot, t, k).wait()
        return carry

    lax.fori_loop(0, COMBINE_ROWS, drain, 0, unroll=2)

    rows = rows_scr.at[slot]
    d = x_ref.shape[1]
    half = d // 2
    gates = gate_ref[...]
    gate_cols = [gates[:, k:k + 1] for k in range(TOP_K)]
    sumsq = None
    for c in range(SUBLANES):
        f_lo = None
        f_hi = None
        for k in range(TOP_K):
            lo, hi = _unpack_bf16_pair(_tile_chunk(rows.at[k], c, COMBINE_ROWS))
            f_lo = gate_cols[k] * lo if f_lo is None else f_lo + gate_cols[k] * lo
            f_hi = gate_cols[k] * hi if f_hi is None else f_hi + gate_cols[k] * hi
        sl_lo = slice(c * LANES, (c + 1) * LANES)
        sl_hi = slice(half + c * LANES, half + (c + 1) * LANES)
        y_lo = x_ref[:, sl_lo] + g2_ref[:, sl_lo] * f_lo
        y_hi = x_ref[:, sl_hi] + g2_ref[:, sl_hi] * f_hi
        o_ref[:, sl_lo] = y_lo
        o_ref[:, sl_hi] = y_hi
        if final_norm:
            part = jnp.sum(y_lo * y_lo + y_hi * y_hi, axis=-1, keepdims=True)
            sumsq = part if sumsq is None else sumsq + part
    if final_norm:
        o_ref[...] = o_ref[...] * lax.rsqrt(sumsq / d + EPS) * nf_ref[...]


def _combine(dest_flat, gates, x_all, mods, layer, group_of_tile_c, y_slots, n_tokens, norm_final, final_norm):
    d = x_all.shape[1]
    n_steps = n_tokens // COMBINE_ROWS
    return pl.pallas_call(
        functools.partial(_combine_kernel, final_norm),
        grid=(n_steps,),
        in_specs=[
            pl.BlockSpec((COMBINE_ROWS * TOP_K,), lambda i: (i,), memory_space=pltpu.SMEM),
            pl.BlockSpec((COMBINE_ROWS * TOP_K,), lambda i: (jnp.minimum(i + 1, n_steps - 1),),
                         memory_space=pltpu.SMEM),
            pl.BlockSpec((COMBINE_ROWS, LANES), lambda i: (i, 0)),
            pl.BlockSpec((COMBINE_ROWS, d), lambda i: (i, 0)),
            pl.BlockSpec((None, None, 1, d), lambda i: (layer, group_of_tile_c(i), 0, 5)),
            pl.BlockSpec((1, d), lambda i: (0, 0)),
            pl.BlockSpec(memory_space=pl.ANY),
        ],
        out_specs=pl.BlockSpec((COMBINE_ROWS, d), lambda i: (i, 0)),
        out_shape=jax.ShapeDtypeStruct((n_tokens, d), F32),
        scratch_shapes=[pltpu.VMEM((2, TOP_K, COMBINE_ROWS * SUBLANES, LANES), U32),
                        pltpu.SemaphoreType.DMA((2,))],
        compiler_params=_cparams(("arbitrary",)),
        name="combine",
    )(dest_flat, dest_flat, gates, x_all, mods, norm_final, y_slots)


def _pool_matrices(seg_len, tile):
    pos = jnp.arange(tile)
    seg = pos // seg_len
    p = pos % seg_len
    mats = []
    for w in POOL_WINDOWS:
        left = w // 2
        right = w - 1 - left
        lo = jnp.clip(p - left, 0, seg_len)
        hi = jnp.clip(p + right + 1, 0, seg_len)
        inside = ((seg[:, None] == seg[None, :]) & (p[None, :] >= lo[:, None]) & (p[None, :] < hi[:, None]))
        mats.append(jnp.where(inside, 1.0 / (hi - lo).astype(F32)[:, None], 0.0))
    return jnp.stack(mats).astype(BF16)


def _routing_plan(top_idx, rank, counts, n_blocks):
    experts = jnp.arange(N_EXPERTS, dtype=I32)
    blocks_e = (counts + MOE_ROWS - 1) // MOE_ROWS
    blk_end = jnp.cumsum(blocks_e)
    blk_start = blk_end - blocks_e
    start_e = blk_start * MOE_ROWS
    start_of = jnp.sum(jnp.where(top_idx[..., None] == experts, start_e, 0), axis=-1)
    dest = (start_of + rank).reshape(-1).astype(I32)
    blk = jnp.arange(n_blocks, dtype=I32)
    block_e = jnp.minimum(jnp.sum((blk[:, None] >= blk_end[None, :]).astype(I32), axis=-1), N_EXPERTS - 1)
    in_e = block_e[:, None] == experts
    local = blk - jnp.sum(jnp.where(in_e, blk_start, 0), axis=-1)
    rows_e = jnp.sum(jnp.where(in_e, counts, 0), axis=-1)
    block_rows = jnp.clip(rows_e - local * MOE_ROWS, 0, MOE_ROWS).astype(I32)
    n_active = blk_end[-1:].astype(I32)
    return dest, block_e.astype(I32), block_rows, n_active


def kernel(x, c, ctx, c_ctx, w_mod, b_mod, norm_mix, w_in, b_gate, head_gain, w_pool, pool_scale,
           w_out, norm_ffn, router_w, router_b, w_gate_up, b_gate_up, w_down, b_down, norm_final):
    b, s, d = x.shape
    ctx_len = ctx.shape[1]
    depth = w_mod.shape[0]
    n_lat = b * s
    n_ctx_rows = b * ctx_len
    t_all = n_lat + n_ctx_rows
    d_main = w_in.shape[2] - 4 * N_HEADS
    assert s % ROW_TILE == 0 and n_ctx_rows % ROW_TILE == 0 and ROW_TILE % ctx_len == 0
    assert ROW_TILE % GRID_W == 0 and s % CHUNK == 0 and ctx_len % CHUNK == 0
    assert b + 1 <= MOD_ROWS

    x_all = jnp.concatenate([x.reshape(n_lat, d), ctx.reshape(n_ctx_rows, d)], axis=0)
    c_rows = jnp.zeros((MOD_ROWS, d), F32).at[:b].set(c).at[b].set(c_ctx)
    mods = _modulation(c_rows, w_mod, b_mod).reshape(depth, MOD_ROWS, 1, N_MOD * d)

    tiles_per_batch = s // ROW_TILE
    group_of_tile = lambda i: jnp.minimum(i // tiles_per_batch, b)
    tile_kind = lambda i: (i >= n_lat // ROW_TILE).astype(I32)
    group_of_tile_c = lambda i: jnp.minimum(i // (s // COMBINE_ROWS), b)
    pmats = jnp.stack([_pool_matrices(GRID_W, ROW_TILE), _pool_matrices(ctx_len, ROW_TILE)])

    out = None
    for l in range(depth):
        last = l == depth - 1
        w_main = w_in[l, :, :d_main].astype(BF16)
        w_gate = jnp.pad(w_in[l, :, d_main:], ((0, 0), (0, LANES - 4 * N_HEADS))).astype(BF16)
        bias_row = jnp.pad(b_gate[l].reshape(1, 4 * N_HEADS), ((0, 0), (0, LANES - 4 * N_HEADS)))
        p_main, gates = _in_proj(x_all, mods, l, group_of_tile, norm_mix[l].reshape(1, d), w_main, w_gate)
        hf, hb = _mlstm(p_main, gates, bias_row, b, s, ctx_len)
        rw = jnp.pad(router_w[l], ((0, 0), (0, LANES - N_EXPERTS))).astype(BF16)
        rb = jnp.pad(router_b[l].reshape(1, N_EXPERTS), ((0, 0), (0, LANES - N_EXPERTS)))
        x_mid, h2, logits = _mix_out(
            x_all, p_main, hf, hb, mods, l, group_of_tile, tile_kind,
            head_gain[l].reshape(1, -1), pmats, w_pool[l].astype(BF16), pool_scale[l].reshape(1, -1),
            w_out[l].astype(BF16), norm_ffn[l].reshape(1, d), rw, rb)

        n_tok = n_lat if last else t_all
        idx_pad, gate_pad, count_pad = _route(logits, n_tok)
        n_assign = n_tok * TOP_K
        n_blocks = -(-n_assign // MOE_ROWS) + N_EXPERTS
        dest, block_e, block_rows, n_active = _routing_plan(
            idx_pad[:, :TOP_K], idx_pad[:, TOP_K:2 * TOP_K], count_pad[0, :N_EXPERTS], n_blocks)
        slots = _dispatch(dest, h2, n_tok, n_blocks * MOE_ROWS)
        y_slots = _moe(slots, block_e, block_rows, n_active, l, w_gate_up, b_gate_up, w_down, b_down)
        x_next = _combine(dest, gate_pad, x_mid, mods, l, group_of_tile_c, y_slots, n_tok,
                          norm_final.reshape(1, d), last)
        if last:
            out = x_next.reshape(b, s, d)
        else:
            x_all = x_next
    return out
```

```python
import functools

import jax
import jax.numpy as jnp
from jax import lax
from jax.experimental import pallas as pl
from jax.experimental.pallas import tpu as pltpu

F32 = jnp.float32
BF16 = jnp.bfloat16
U32 = jnp.uint32
I32 = jnp.int32

GRID_W = 64
N_HEADS = 4
HEAD_V = 256
HEAD_QK = 128
CHUNK = 128
POOL_WINDOWS = (2, 4, 8, 16)
N_POOL_GROUPS = 4
POOL_GROUP = 256
N_EXPERTS = 32
TOP_K = 4
SWIGLU_ALPHA = 1.702
SWIGLU_LIMIT = 7.0
N_MOD = 6
EPS = 1e-6
LANES = 128
MOD_ROWS = 8

SUBLANES = 8

ROW_TILE = 512
MOE_ROWS = 1024
MOE_FF_TILE = 512
DISPATCH_ROWS = 256
COMBINE_ROWS = 128
VMEM_LIMIT = 56 * 1024 * 1024
MOE_VMEM_LIMIT = 60 * 1024 * 1024


def _cparams(sem, vmem_limit=VMEM_LIMIT):
    return pltpu.CompilerParams(dimension_semantics=sem, vmem_limit_bytes=vmem_limit)


def _pack_bf16_pair(lo, hi):
    lo_bits = pltpu.bitcast(lo.astype(BF16).astype(F32), U32) >> 16
    hi_bits = pltpu.bitcast(hi.astype(BF16).astype(F32), U32) & jnp.uint32(0xFFFF0000)
    return hi_bits | lo_bits


def _unpack_bf16_pair(w):
    lo = pltpu.bitcast(w << 16, F32)
    hi = pltpu.bitcast(w & jnp.uint32(0xFFFF0000), F32)
    return lo, hi


def _tile_chunk(ref, c, n_rows):
    return ref[pl.ds(c, n_rows, stride=SUBLANES), :]


def _store_token_tiles(ref, read_cols, n_rows, d):
    half = d // 2
    assert half == SUBLANES * LANES
    for c in range(SUBLANES):
        lo = read_cols(slice(c * LANES, (c + 1) * LANES))
        hi = read_cols(slice(half + c * LANES, half + (c + 1) * LANES))
        ref[pl.ds(c, n_rows, stride=SUBLANES), :] = _pack_bf16_pair(lo, hi)


def _mod_kernel(c_ref, w_ref, b_ref, o_ref):
    c = c_ref[...]
    s = c * jax.nn.sigmoid(c)
    o_ref[...] = jnp.dot(s, w_ref[...], preferred_element_type=F32,
                         precision=lax.Precision.HIGHEST) + b_ref[...]


def _modulation(c_rows, w_mod, b_mod):
    depth, d, n = w_mod.shape
    tn = 1024
    return pl.pallas_call(
        _mod_kernel,
        grid=(depth, n // tn),
        in_specs=[
            pl.BlockSpec((MOD_ROWS, d), lambda l, j: (0, 0)),
            pl.BlockSpec((None, d, tn), lambda l, j: (l, 0, j)),
            pl.BlockSpec((None, 1, tn), lambda l, j: (l, 0, j)),
        ],
        out_specs=pl.BlockSpec((None, MOD_ROWS, tn), lambda l, j: (l, 0, j)),
        out_shape=jax.ShapeDtypeStruct((depth, MOD_ROWS, n), F32),
        compiler_params=_cparams(("arbitrary", "arbitrary")),
        name="modulation",
    )(c_rows, w_mod, b_mod.reshape(depth, 1, n))


def _in_proj_kernel(n_latent_tiles, *refs):
    if n_latent_tiles is None:
        x_ref, sh_ref, sc_ref, g_ref, w_ref, wg_ref, p_ref, gate_ref, h_scr = refs
    else:
        x_ref, ctx_ref, sh_ref, sc_ref, g_ref, w_ref, wg_ref, p_ref, gate_ref, xall_ref, h_scr = refs

    @pl.when(pl.program_id(1) == 0)
    def _():
        x = x_ref[...]
        if n_latent_tiles is not None:
            x = jnp.where(pl.program_id(0) >= n_latent_tiles, ctx_ref[...], x)
            xall_ref[...] = x
        y = x * lax.rsqrt(jnp.mean(x * x, axis=-1, keepdims=True) + EPS) * g_ref[...]
        h = (y * (1.0 + sc_ref[...]) + sh_ref[...]).astype(BF16)
        h_scr[...] = h
        gate_ref[...] = jnp.dot(h, wg_ref[...], preferred_element_type=F32)

    p_ref[...] = jnp.dot(h_scr[...], w_ref[...], preferred_element_type=F32).astype(BF16)


def _in_proj(x_rows, ctx_rows, mods, layer, group_of_tile, norm_g, w_main, w_gate):
    d = x_rows.shape[1]
    split = ctx_rows is not None
    n_latent_tiles = x_rows.shape[0] // ROW_TILE if split else None
    t_all = x_rows.shape[0] + (ctx_rows.shape[0] if split else 0)
    n_main = w_main.shape[1]
    tn = 2048
    mod_spec = lambda k: pl.BlockSpec(
        (None, None, 1, d), lambda i, j: (layer, group_of_tile(i), 0, k))
    if split:
        assert ctx_rows.shape[0] == ROW_TILE
        x_specs = [pl.BlockSpec((ROW_TILE, d), lambda i, j: (jnp.minimum(i, n_latent_tiles - 1), 0)),
                   pl.BlockSpec((ROW_TILE, d), lambda i, j: (0, 0))]
        operands = (x_rows, ctx_rows)
    else:
        x_specs = [pl.BlockSpec((ROW_TILE, d), lambda i, j: (i, 0))]
        operands = (x_rows,)
    out_specs = [
        pl.BlockSpec((ROW_TILE, tn), lambda i, j: (i, j)),
        pl.BlockSpec((ROW_TILE, LANES), lambda i, j: (i, 0)),
    ]
    out_shape = [
        jax.ShapeDtypeStruct((t_all, n_main), BF16),
        jax.ShapeDtypeStruct((t_all, LANES), F32),
    ]
    if split:
        out_specs.append(pl.BlockSpec((ROW_TILE, d), lambda i, j: (i, 0)))
        out_shape.append(jax.ShapeDtypeStruct((t_all, d), F32))
    return pl.pallas_call(
        functools.partial(_in_proj_kernel, n_latent_tiles),
        grid=(t_all // ROW_TILE, n_main // tn),
        in_specs=x_specs + [
            mod_spec(0), mod_spec(1),
            pl.BlockSpec((1, d), lambda i, j: (0, 0)),
            pl.BlockSpec((d, tn), lambda i, j: (0, j)),
            pl.BlockSpec((d, LANES), lambda i, j: (0, 0)),
        ],
        out_specs=out_specs,
        out_shape=out_shape,
        scratch_shapes=[pltpu.VMEM((ROW_TILE, d), BF16)],
        compiler_params=_cparams(("arbitrary", "arbitrary")),
        name="in_proj",
    )(*operands, mods, mods, norm_g, w_main, w_gate)


def _log_sigmoid(x):
    return jnp.minimum(x, 0.0) - jnp.log(1.0 + jnp.exp(-jnp.abs(x)))


def _mlstm_chain(q, k, v, a, a_t, cs, cs_t, col_i, col_f, reverse, c_ref, n_ref, m_ref):
    bc = cs[:, col_f:col_f + 1]
    br = cs_t[col_f:col_f + 1, :]
    ir = a_t[col_i:col_i + 1, :]
    ic = a[:, col_i:col_i + 1]
    t_idx = lax.broadcasted_iota(I32, (CHUNK, CHUNK), 0)
    s_idx = lax.broadcasted_iota(I32, (CHUNK, CHUNK), 1)
    mask = (s_idx >= t_idx) if reverse else (s_idx <= t_idx)
    dmat = jnp.where(mask, bc - br + ir, -jnp.inf)
    m_prev = m_ref[...]
    m_inter = bc + m_prev
    m_t = jnp.maximum(m_inter, jnp.max(dmat, axis=-1, keepdims=True))
    scale = HEAD_QK ** -0.5
    qk = lax.dot_general(q, k, (((1,), (1,)), ((), ())), preferred_element_type=F32)
    s = qk * (scale * jnp.exp(dmat - m_t))
    w_inter = scale * jnp.exp(m_inter - m_t)
    c_old = c_ref[...]
    num = (jnp.dot(s.astype(BF16), v, preferred_element_type=F32)
           + w_inter * jnp.dot(q, c_old.astype(BF16), preferred_element_type=F32))
    qn = jnp.sum(q.astype(F32) * n_ref[...], axis=-1, keepdims=True)
    den = jnp.sum(s, axis=-1, keepdims=True) + w_inter * qn
    h = num / jnp.maximum(jnp.abs(den), jnp.exp(-m_t))

    b_end = bc[0:1, :] if reverse else bc[CHUNK - 1:CHUNK, :]
    g = b_end - bc + ic
    m_new = jnp.maximum(b_end + m_prev, jnp.max(g, axis=0, keepdims=True))
    wg = jnp.exp(g - m_new)
    wc = jnp.exp(b_end + m_prev - m_new)
    kw = k.astype(F32) * wg
    c_ref[...] = wc * c_old + lax.dot_general(
        kw.astype(BF16), v, (((0,), (0,)), ((), ())), preferred_element_type=F32)
    n_ref[...] = wc * n_ref[...] + jnp.sum(kw, axis=0, keepdims=True)
    m_ref[...] = m_new
    return h


def _mlstm_kernel(batch, *refs):
    in_refs = refs[:8 * batch]
    bias_ref, hf_ref, hb_ref, c_scr, n_scr, m_scr = refs[8 * batch:]

    @pl.when(pl.program_id(0) == 0)
    def _():
        c_scr[...] = jnp.zeros_like(c_scr)
        n_scr[...] = jnp.zeros_like(n_scr)
        m_scr[...] = jnp.zeros_like(m_scr)

    lane = lax.broadcasted_iota(I32, (CHUNK, LANES), 1)
    is_f = ((lane >= N_HEADS) & (lane < 2 * N_HEADS)) | ((lane >= 3 * N_HEADS) & (lane < 4 * N_HEADS))
    t_idx = lax.broadcasted_iota(I32, (CHUNK, CHUNK), 0)
    s_idx = lax.broadcasted_iota(I32, (CHUNK, CHUNK), 1)

    for b in range(batch):
        for direction, out_ref in enumerate((hf_ref, hb_ref)):
            q_ref, k_ref, v_ref, g_ref = in_refs[8 * b + 4 * direction:8 * b + 4 * direction + 4]
            reverse = direction == 1
            graw = g_ref[...] + bias_ref[...]
            a = jnp.where(is_f, _log_sigmoid(graw), graw)
            tri = ((s_idx >= t_idx) if reverse else (s_idx <= t_idx)).astype(F32)
            cs = jnp.dot(tri, a, preferred_element_type=F32, precision=lax.Precision.HIGHEST)
            a_t = a.T
            cs_t = cs.T
            for hd in range(N_HEADS):
                col_i = 2 * direction * N_HEADS + hd
                col_f = col_i + N_HEADS
                h = _mlstm_chain(
                    q_ref[:, hd * HEAD_QK:(hd + 1) * HEAD_QK],
                    k_ref[:, hd * HEAD_QK:(hd + 1) * HEAD_QK],
                    v_ref[:, hd * HEAD_V:(hd + 1) * HEAD_V],
                    a, a_t, cs, cs_t, col_i, col_f, reverse,
                    c_scr.at[b, direction, hd], n_scr.at[b, direction, hd], m_scr.at[b, direction, hd])
                out_ref[b, :, hd * HEAD_V:(hd + 1) * HEAD_V] = h.astype(BF16)


def _mlstm(p_main, gates, bias_row, batch, seq, ctx_len):
    n_ctx = ctx_len // CHUNK
    n_seq = seq // CHUNK
    ctx_base = batch * seq // CHUNK
    d_qk = N_HEADS * HEAD_QK
    d_v = N_HEADS * HEAD_V

    def fwd_pos(i):
        return jnp.where(i < n_ctx, n_seq + i, i - n_ctx)

    def bwd_pos(i):
        return jnp.where(i < n_ctx, n_seq + (n_ctx - 1 - i), n_seq - 1 - (i - n_ctx))

    def row_blk(b, pos):
        return jnp.where(pos < n_seq, b * n_seq + pos, ctx_base + b * n_ctx + (pos - n_seq))

    def specs(b, pos_of):
        return [
            pl.BlockSpec((CHUNK, d_qk), lambda i: (row_blk(b, pos_of(i)), 0)),
            pl.BlockSpec((CHUNK, d_qk), lambda i: (row_blk(b, pos_of(i)), 1)),
            pl.BlockSpec((CHUNK, d_v), lambda i: (row_blk(b, pos_of(i)), 1)),
            pl.BlockSpec((CHUNK, LANES), lambda i: (row_blk(b, pos_of(i)), 0)),
        ]

    in_specs = []
    operands = []
    for b in range(batch):
        in_specs += specs(b, fwd_pos) + specs(b, bwd_pos)
        operands += [p_main, p_main, p_main, gates] * 2
    return pl.pallas_call(
        functools.partial(_mlstm_kernel, batch),
        grid=(n_ctx + n_seq,),
        in_specs=in_specs + [pl.BlockSpec((1, LANES), lambda i: (0, 0))],
        out_specs=[
            pl.BlockSpec((batch, CHUNK, d_v), lambda i: (0, fwd_pos(i), 0)),
            pl.BlockSpec((batch, CHUNK, d_v), lambda i: (0, bwd_pos(i), 0)),
        ],
        out_shape=[jax.ShapeDtypeStruct((batch, seq + ctx_len, d_v), BF16)] * 2,
        scratch_shapes=[
            pltpu.VMEM((batch, 2, N_HEADS, HEAD_QK, HEAD_V), F32),
            pltpu.VMEM((batch, 2, N_HEADS, 1, HEAD_QK), F32),
            pltpu.VMEM((batch, 2, N_HEADS, 1, 1), F32),
        ],
        compiler_params=_cparams(("arbitrary",)),
        name="mlstm",
    )(*operands, bias_row)


def _mix_out_kernel(n_latent_tiles, x_ref, o_ref, u_ref, hf_ref, hb_ref, hfc_ref, hbc_ref, gain_ref, pmat_ref,
                    wpool_ref, pscale_ref, wout_ref, g1_ref, sh2_ref, sc2_ref, nffn_ref, rw_ref, rb_ref,
                    xnew_ref, h2_ref, logit_ref, mix_scr):
    is_ctx = pl.program_id(0) >= n_latent_tiles
    n_rows = x_ref.shape[0]
    for hd in range(N_HEADS):
        sl = slice(hd * HEAD_V, (hd + 1) * HEAD_V)
        h_f = jnp.where(is_ctx, hfc_ref[:, :, sl].reshape(n_rows, HEAD_V), hf_ref[:, sl])
        h_b = jnp.where(is_ctx, hbc_ref[:, :, sl].reshape(n_rows, HEAD_V), hb_ref[:, sl])
        h = h_f.astype(F32) + h_b.astype(F32)
        hn = h * lax.rsqrt(jnp.mean(h * h, axis=-1, keepdims=True) + EPS) * gain_ref[:, sl]
        mix_scr[:, sl] = (jax.nn.sigmoid(o_ref[:, sl].astype(F32)) * hn).astype(BF16)
    d_mlstm = N_HEADS * HEAD_V
    for g in range(N_POOL_GROUPS):
        sl = slice(g * POOL_GROUP, (g + 1) * POOL_GROUP)
        u = u_ref[:, sl]
        dlt = jnp.dot(pmat_ref[g], u, preferred_element_type=F32) - u.astype(F32)
        y = jnp.dot(dlt.astype(BF16), wpool_ref[g], preferred_element_type=F32) * pscale_ref[:, sl]
        mix_scr[:, d_mlstm + g * POOL_GROUP:d_mlstm + (g + 1) * POOL_GROUP] = y.astype(BF16)
    acc = jnp.dot(mix_scr[...], wout_ref[...], preferred_element_type=F32)
    xn = x_ref[...] + g1_ref[...] * acc
    xnew_ref[...] = xn
    y2 = xn * lax.rsqrt(jnp.mean(xn * xn, axis=-1, keepdims=True) + EPS) * nffn_ref[...]
    h2 = y2 * (1.0 + sc2_ref[...]) + sh2_ref[...]
    _store_token_tiles(h2_ref, lambda sl: h2[:, sl], h2.shape[0], h2.shape[1])
    logit_ref[...] = jnp.dot(h2.astype(BF16), rw_ref[...], preferred_element_type=F32) + rb_ref[...]


def _mix_out(x_all, p_main, hf, hb, ctx_len, mods, layer, group_of_tile, tile_kind, gain, pmats, w_pool,
             pool_scale, w_out, norm_ffn, router_w, router_b):
    t_all, d = x_all.shape
    d_half = d // 2
    batch, positions, d_v = hf.shape
    seq = positions - ctx_len
    tiles_per_batch = seq // ROW_TILE
    n_latent_tiles = batch * tiles_per_batch
    assert batch * ctx_len == ROW_TILE and n_latent_tiles + 1 == t_all // ROW_TILE and seq % ctx_len == 0
    mod_spec = lambda k: pl.BlockSpec(
        (None, None, 1, d), lambda i: (layer, group_of_tile(i), 0, k))
    const = lambda shape: pl.BlockSpec(shape, lambda i: (0,) * len(shape), pipeline_mode=pl.Buffered(1))
    latent_spec = pl.BlockSpec(
        (None, ROW_TILE, d_v), lambda i: (jnp.minimum(i // tiles_per_batch, batch - 1), i % tiles_per_batch, 0))
    ctx_spec = pl.BlockSpec((batch, ctx_len, d_v), lambda i: (0, seq // ctx_len, 0), pipeline_mode=pl.Buffered(1))
    return pl.pallas_call(
        functools.partial(_mix_out_kernel, n_latent_tiles),
        grid=(t_all // ROW_TILE,),
        in_specs=[
            pl.BlockSpec((ROW_TILE, d), lambda i: (i, 0)),
            pl.BlockSpec((ROW_TILE, d_half), lambda i: (i, 2)),
            pl.BlockSpec((ROW_TILE, d_half), lambda i: (i, 3)),
            latent_spec, latent_spec, ctx_spec, ctx_spec,
            const((1, d_half)),
            pl.BlockSpec((None, N_POOL_GROUPS, ROW_TILE, ROW_TILE), lambda i: (tile_kind(i), 0, 0, 0)),
            const((N_POOL_GROUPS, POOL_GROUP, POOL_GROUP)),
            const((1, d_half)),
            const((d, d)),
            mod_spec(2), mod_spec(3), mod_spec(4),
            const((1, d)),
            const((d, LANES)),
            const((1, LANES)),
        ],
        out_specs=[
            pl.BlockSpec((ROW_TILE, d), lambda i: (i, 0)),
            pl.BlockSpec((ROW_TILE * SUBLANES, LANES), lambda i: (i, 0)),
            pl.BlockSpec((ROW_TILE, LANES), lambda i: (i, 0)),
        ],
        out_shape=[
            jax.ShapeDtypeStruct((t_all, d), F32),
            jax.ShapeDtypeStruct((t_all * SUBLANES, LANES), U32),
            jax.ShapeDtypeStruct((t_all, LANES), F32),
        ],
        scratch_shapes=[pltpu.VMEM((ROW_TILE, d), BF16)],
        compiler_params=_cparams(("arbitrary",)),
        name="mix_out",
    )(x_all, p_main, p_main, hf, hb, hf, hb, gain, pmats, w_pool, pool_scale, w_out,
      mods, mods, mods, norm_ffn, router_w, router_b)


def _route_kernel(logit_ref, idx_ref, gate_ref, count_ref, run_scr):
    @pl.when(pl.program_id(0) == 0)
    def _():
        run_scr[...] = jnp.zeros_like(run_scr)

    lg = logit_ref[...]
    n_rows = lg.shape[0]
    lane = lax.broadcasted_iota(I32, lg.shape, 1).astype(F32)
    lg = jnp.where(lane < N_EXPERTS, lg, -jnp.inf)
    vals, idxs, onehots = [], [], []
    for _ in range(TOP_K):
        mx = jnp.max(lg, axis=-1, keepdims=True)
        ix = jnp.min(jnp.where(lg == mx, lane, float(LANES)), axis=-1, keepdims=True)
        hit = lane == ix
        vals.append(mx)
        idxs.append(ix)
        onehots.append(hit.astype(F32))
        lg = jnp.where(hit, -jnp.inf, lg)
    es = [jnp.exp(v - vals[0]) for v in vals]
    tot = es[0] + es[1] + es[2] + es[3]

    onehot = onehots[0] + onehots[1] + onehots[2] + onehots[3]
    r_idx = lax.broadcasted_iota(I32, (n_rows, n_rows), 0)
    c_idx = lax.broadcasted_iota(I32, (n_rows, n_rows), 1)
    earlier = (c_idx < r_idx).astype(BF16)
    before = jnp.dot(earlier, onehot.astype(BF16), preferred_element_type=F32) + run_scr[...]
    run_scr[...] += jnp.sum(onehot, axis=0, keepdims=True)

    idx_out = jnp.zeros(lg.shape, F32)
    gate_out = jnp.zeros(lg.shape, F32)
    for k in range(TOP_K):
        rank = jnp.sum(onehots[k] * before, axis=-1, keepdims=True)
        idx_out = jnp.where(lane == k, idxs[k], idx_out)
        idx_out = jnp.where(lane == TOP_K + k, rank, idx_out)
        gate_out = jnp.where(lane == k, es[k] / tot, gate_out)
    idx_ref[...] = idx_out.astype(I32)
    gate_ref[...] = gate_out
    count_ref[...] = jnp.broadcast_to(run_scr[...], count_ref.shape).astype(I32)


def _route(logits, n_tokens):
    spec = pl.BlockSpec((ROW_TILE, LANES), lambda i: (i, 0))
    return pl.pallas_call(
        _route_kernel,
        grid=(n_tokens // ROW_TILE,),
        in_specs=[spec],
        out_specs=[spec, spec, pl.BlockSpec((SUBLANES, LANES), lambda i: (0, 0))],
        out_shape=[jax.ShapeDtypeStruct((n_tokens, LANES), I32), jax.ShapeDtypeStruct((n_tokens, LANES), F32),
                   jax.ShapeDtypeStruct((SUBLANES, LANES), I32)],
        scratch_shapes=[pltpu.VMEM((1, LANES), F32)],
        compiler_params=_cparams(("arbitrary",)),
        name="route",
    )(logits)


def _dispatch_kernel(dest_ref, h_ref, slots_hbm, sem):
    def tile_copy(t, k):
        src = pl.multiple_of(t * SUBLANES, SUBLANES)
        dst = pl.multiple_of(dest_ref[t * TOP_K + k] * SUBLANES, SUBLANES)
        return pltpu.make_async_copy(
            h_ref.at[pl.ds(src, SUBLANES)], slots_hbm.at[pl.ds(dst, SUBLANES)], sem)

    def issue(t, carry):
        for k in range(TOP_K):
            tile_copy(t, k).start(priority=k % 2)
        return carry

    lax.fori_loop(0, DISPATCH_ROWS, issue, 0, unroll=2)

    def drain(t, carry):
        for k in range(TOP_K):
            tile_copy(t, k).wait()
        return carry

    lax.fori_loop(0, DISPATCH_ROWS, drain, 0, unroll=2)


def _dispatch(dest_flat, h_tiles, n_tokens, n_slots):
    return pl.pallas_call(
        _dispatch_kernel,
        grid=(n_tokens // DISPATCH_ROWS,),
        in_specs=[
            pl.BlockSpec((DISPATCH_ROWS * TOP_K,), lambda i: (i,), memory_space=pltpu.SMEM),
            pl.BlockSpec((DISPATCH_ROWS * SUBLANES, LANES), lambda i: (i, 0)),
        ],
        out_specs=pl.BlockSpec(memory_space=pl.ANY),
        out_shape=jax.ShapeDtypeStruct((n_slots * SUBLANES, LANES), U32),
        scratch_shapes=[pltpu.SemaphoreType.DMA(())],
        compiler_params=_cparams(("arbitrary",)),
        name="dispatch",
    )(dest_flat, h_tiles)


def _moe_kernel(be_ref, nv_ref, na_ref, x_ref, wg_ref, wu_ref, bg_ref, bu_ref, wd_ref, bd_ref, o_ref,
                x_scr, acc_scr):
    i = pl.program_id(0)
    j = pl.program_id(1)
    n_j = pl.num_programs(1)
    n_rows, d = acc_scr.shape
    half = d // 2

    @pl.when(i < na_ref[0])
    def _():
        @pl.when(j == 0)
        def _():
            live = lax.broadcasted_iota(I32, (n_rows, LANES), 0) < nv_ref[i]
            for c in range(SUBLANES):
                words = jnp.where(live, _tile_chunk(x_ref, c, n_rows), jnp.uint32(0))
                lo, hi = _unpack_bf16_pair(words)
                x_scr[:, c * LANES:(c + 1) * LANES] = lo.astype(BF16)
                x_scr[:, half + c * LANES:half + (c + 1) * LANES] = hi.astype(BF16)

            @pl.when(i == 0)
            def _():
                acc_scr[...] = jnp.zeros_like(acc_scr)

        def expert_mlp(m):
            x = x_scr[:m, :]
            gl = jnp.dot(x, wg_ref[...].astype(BF16), preferred_element_type=F32) + bg_ref[...]
            ul = jnp.dot(x, wu_ref[...].astype(BF16), preferred_element_type=F32) + bu_ref[...]
            x_glu = jnp.minimum(gl, SWIGLU_LIMIT)
            x_lin = jnp.clip(ul, -SWIGLU_LIMIT, SWIGLU_LIMIT)
            act = x_glu * jax.nn.sigmoid(SWIGLU_ALPHA * x_glu) * (x_lin + 1.0)
            part = jnp.dot(act.astype(BF16), wd_ref[...].astype(BF16), preferred_element_type=F32)
            base = jnp.where(j == 0, jnp.broadcast_to(bd_ref[...], (m, d)), acc_scr[:m, :])
            acc_scr[:m, :] = base + part

        quarter = n_rows // 4
        for q in range(1, 5):
            @pl.when((nv_ref[i] > (q - 1) * quarter) & (nv_ref[i] <= q * quarter))
            def _(q=q):
                expert_mlp(q * quarter)

        @pl.when(j == n_j - 1)
        def _():
            _store_token_tiles(o_ref, lambda sl: acc_scr[:, sl], n_rows, d)


def _moe(slots, block_e, block_rows, n_active, layer, w_gate_up, b_gate_up, w_down, b_down):
    n_slots = slots.shape[0] // SUBLANES
    d = w_down.shape[3]
    depth, n_e, _, two_f = w_gate_up.shape
    d_ff = two_f // 2
    n_j = d_ff // MOE_FF_TILE
    n_blocks = n_slots // MOE_ROWS

    def blk(i, na):
        return jnp.minimum(i, na[0] - 1)

    def jj(i, j, na):
        return jnp.where(i < na[0], j, n_j - 1)

    grid_spec = pltpu.PrefetchScalarGridSpec(
        num_scalar_prefetch=3,
        grid=(n_blocks, n_j),
        in_specs=[
            pl.BlockSpec((MOE_ROWS * SUBLANES, LANES), lambda i, j, be, nv, na: (blk(i, na), 0)),
            pl.BlockSpec((None, None, d, MOE_FF_TILE),
                         lambda i, j, be, nv, na: (layer, be[blk(i, na)], 0, jj(i, j, na))),
            pl.BlockSpec((None, None, d, MOE_FF_TILE),
                         lambda i, j, be, nv, na: (layer, be[blk(i, na)], 0, n_j + jj(i, j, na))),
            pl.BlockSpec((None, None, 1, MOE_FF_TILE),
                         lambda i, j, be, nv, na: (layer, be[blk(i, na)], 0, jj(i, j, na))),
            pl.BlockSpec((None, None, 1, MOE_FF_TILE),
                         lambda i, j, be, nv, na: (layer, be[blk(i, na)], 0, n_j + jj(i, j, na))),
            pl.BlockSpec((None, None, MOE_FF_TILE, d),
                         lambda i, j, be, nv, na: (layer, be[blk(i, na)], jj(i, j, na), 0)),
            pl.BlockSpec((None, None, 1, d), lambda i, j, be, nv, na: (layer, be[blk(i, na)], 0, 0)),
        ],
        out_specs=pl.BlockSpec((MOE_ROWS * SUBLANES, LANES), lambda i, j, be, nv, na: (blk(i, na), 0)),
        scratch_shapes=[pltpu.VMEM((MOE_ROWS, d), BF16), pltpu.VMEM((MOE_ROWS, d), F32)],
    )
    return pl.pallas_call(
        _moe_kernel,
        grid_spec=grid_spec,
        out_shape=jax.ShapeDtypeStruct((n_slots * SUBLANES, LANES), U32),
        compiler_params=_cparams(("arbitrary", "arbitrary"), MOE_VMEM_LIMIT),
        name="moe",
    )(block_e, block_rows, n_active, slots, w_gate_up, w_gate_up,
      b_gate_up.reshape(depth, n_e, 1, two_f), b_gate_up.reshape(depth, n_e, 1, two_f),
      w_down, b_down.reshape(depth, n_e, 1, d))


def _combine_kernel(final_norm, dest_ref, dest_next_ref, gate_ref, x_ref, g2_ref, nf_ref, y_hbm, o_ref,
                    rows_scr, sems):
    i = pl.program_id(0)
    slot = i % 2

    def tile_copy(d_ref, buf, t, k):
        src = pl.multiple_of(d_ref[t * TOP_K + k] * SUBLANES, SUBLANES)
        dst = pl.multiple_of(t * SUBLANES, SUBLANES)
        return pltpu.make_async_copy(
            y_hbm.at[pl.ds(src, SUBLANES)], rows_scr.at[buf, k, pl.ds(dst, SUBLANES)], sems.at[buf])

    def start_tile(d_ref, buf):
        def issue(t, carry):
            for k in range(TOP_K):
                tile_copy(d_ref, buf, t, k).start(priority=k % 2)
            return carry
        lax.fori_loop(0, COMBINE_ROWS, issue, 0, unroll=2)

    @pl.when(i == 0)
    def _():
        start_tile(dest_ref, 0)

    @pl.when(i + 1 < pl.num_programs(0))
    def _():
        start_tile(dest_next_ref, 1 - slot)

    def drain(t, carry):
        for k in range(TOP_K):
            tile_copy(dest_ref, slot, t, k).wait()
        return carry

    lax.fori_loop(0, COMBINE_ROWS, drain, 0, unroll=2)

    rows = rows_scr.at[slot]
    d = x_ref.shape[1]
    half = d // 2
    gates = gate_ref[...]
    gate_cols = [gates[:, k:k + 1] for k in range(TOP_K)]
    sumsq = None
    for c in range(SUBLANES):
        f_lo = None
        f_hi = None
        for k in range(TOP_K):
            lo, hi = _unpack_bf16_pair(_tile_chunk(rows.at[k], c, COMBINE_ROWS))
            f_lo = gate_cols[k] * lo if f_lo is None else f_lo + gate_cols[k] * lo
            f_hi = gate_cols[k] * hi if f_hi is None else f_hi + gate_cols[k] * hi
        sl_lo = slice(c * LANES, (c + 1) * LANES)
        sl_hi = slice(half + c * LANES, half + (c + 1) * LANES)
        y_lo = x_ref[:, sl_lo] + g2_ref[:, sl_lo] * f_lo
        y_hi = x_ref[:, sl_hi] + g2_ref[:, sl_hi] * f_hi
        o_ref[:, sl_lo] = y_lo
        o_ref[:, sl_hi] = y_hi
        if final_norm:
            part = jnp.sum(y_lo * y_lo + y_hi * y_hi, axis=-1, keepdims=True)
            sumsq = part if sumsq is None else sumsq + part
    if final_norm:
        o_ref[...] = o_ref[...] * lax.rsqrt(sumsq / d + EPS) * nf_ref[...]


def _combine(dest_flat, gates, x_all, mods, layer, group_of_tile_c, y_slots, n_tokens, norm_final, final_norm):
    d = x_all.shape[1]
    n_steps = n_tokens // COMBINE_ROWS
    return pl.pallas_call(
        functools.partial(_combine_kernel, final_norm),
        grid=(n_steps,),
        in_specs=[
            pl.BlockSpec((COMBINE_ROWS * TOP_K,), lambda i: (i,), memory_space=pltpu.SMEM),
            pl.BlockSpec((COMBINE_ROWS * TOP_K,), lambda i: (jnp.minimum(i + 1, n_steps - 1),),
                         memory_space=pltpu.SMEM),
            pl.BlockSpec((COMBINE_ROWS, LANES), lambda i: (i, 0)),
            pl.BlockSpec((COMBINE_ROWS, d), lambda i: (i, 0)),
            pl.BlockSpec((None, None, 1, d), lambda i: (layer, group_of_tile_c(i), 0, 5)),
            pl.BlockSpec((1, d), lambda i: (0, 0)),
            pl.BlockSpec(memory_space=pl.ANY),
        ],
        out_specs=pl.BlockSpec((COMBINE_ROWS, d), lambda i: (i, 0)),
        out_shape=jax.ShapeDtypeStruct((n_tokens, d), F32),
        scratch_shapes=[pltpu.VMEM((2, TOP_K, COMBINE_ROWS * SUBLANES, LANES), U32),
                        pltpu.SemaphoreType.DMA((2,))],
        compiler_params=_cparams(("arbitrary",)),
        name="combine",
    )(dest_flat, dest_flat, gates, x_all, mods, norm_final, y_slots)


def _pool_matrices(seg_len, tile):
    pos = jnp.arange(tile)
    seg = pos // seg_len
    p = pos % seg_len
    mats = []
    for w in POOL_WINDOWS:
        left = w // 2
        right = w - 1 - left
        lo = jnp.clip(p - left, 0, seg_len)
        hi = jnp.clip(p + right + 1, 0, seg_len)
        inside = ((seg[:, None] == seg[None, :]) & (p[None, :] >= lo[:, None]) & (p[None, :] < hi[:, None]))
        mats.append(jnp.where(inside, 1.0 / (hi - lo).astype(F32)[:, None], 0.0))
    return jnp.stack(mats).astype(BF16)


def _routing_plan(top_idx, rank, counts, n_blocks):
    experts = jnp.arange(N_EXPERTS, dtype=I32)
    blocks_e = (counts + MOE_ROWS - 1) // MOE_ROWS
    blk_end = jnp.cumsum(blocks_e)
    blk_start = blk_end - blocks_e
    start_e = blk_start * MOE_ROWS
    start_of = jnp.sum(jnp.where(top_idx[..., None] == experts, start_e, 0), axis=-1)
    dest = (start_of + rank).reshape(-1).astype(I32)
    blk = jnp.arange(n_blocks, dtype=I32)
    block_e = jnp.minimum(jnp.sum((blk[:, None] >= blk_end[None, :]).astype(I32), axis=-1), N_EXPERTS - 1)
    in_e = block_e[:, None] == experts
    local = blk - jnp.sum(jnp.where(in_e, blk_start, 0), axis=-1)
    rows_e = jnp.sum(jnp.where(in_e, counts, 0), axis=-1)
    block_rows = jnp.clip(rows_e - local * MOE_ROWS, 0, MOE_ROWS).astype(I32)
    n_active = blk_end[-1:].astype(I32)
    return dest, block_e.astype(I32), block_rows, n_active


def kernel(x, c, ctx, c_ctx, w_mod, b_mod, norm_mix, w_in, b_gate, head_gain, w_pool, pool_scale,
           w_out, norm_ffn, router_w, router_b, w_gate_up, b_gate_up, w_down, b_down, norm_final):
    b, s, d = x.shape
    ctx_len = ctx.shape[1]
    depth = w_mod.shape[0]
    n_lat = b * s
    n_ctx_rows = b * ctx_len
    t_all = n_lat + n_ctx_rows
    d_main = w_in.shape[2] - 4 * N_HEADS
    assert s % ROW_TILE == 0 and n_ctx_rows % ROW_TILE == 0 and ROW_TILE % ctx_len == 0
    assert ROW_TILE % GRID_W == 0 and s % CHUNK == 0 and ctx_len % CHUNK == 0
    assert b + 1 <= MOD_ROWS

    x_all = None
    c_rows = jnp.zeros((MOD_ROWS, d), F32).at[:b].set(c).at[b].set(c_ctx)
    mods = _modulation(c_rows, w_mod, b_mod).reshape(depth, MOD_ROWS, 1, N_MOD * d)

    tiles_per_batch = s // ROW_TILE
    group_of_tile = lambda i: jnp.minimum(i // tiles_per_batch, b)
    tile_kind = lambda i: (i >= n_lat // ROW_TILE).astype(I32)
    group_of_tile_c = lambda i: jnp.minimum(i // (s // COMBINE_ROWS), b)
    pmats = jnp.stack([_pool_matrices(GRID_W, ROW_TILE), _pool_matrices(ctx_len, ROW_TILE)])

    out = None
    for l in range(depth):
        last = l == depth - 1
        w_main = w_in[l, :, :d_main].astype(BF16)
        w_gate = jnp.pad(w_in[l, :, d_main:], ((0, 0), (0, LANES - 4 * N_HEADS))).astype(BF16)
        bias_row = jnp.pad(b_gate[l].reshape(1, 4 * N_HEADS), ((0, 0), (0, LANES - 4 * N_HEADS)))
        if x_all is None:
            p_main, gates, x_all = _in_proj(x.reshape(n_lat, d), ctx.reshape(n_ctx_rows, d), mods, l,
                                            group_of_tile, norm_mix[l].reshape(1, d), w_main, w_gate)
        else:
            p_main, gates = _in_proj(x_all, None, mods, l, group_of_tile, norm_mix[l].reshape(1, d),
                                     w_main, w_gate)
        hf, hb = _mlstm(p_main, gates, bias_row, b, s, ctx_len)
        rw = jnp.pad(router_w[l], ((0, 0), (0, LANES - N_EXPERTS))).astype(BF16)
        rb = jnp.pad(router_b[l].reshape(1, N_EXPERTS), ((0, 0), (0, LANES - N_EXPERTS)))
        x_mid, h2, logits = _mix_out(
            x_all, p_main, hf, hb, ctx_len, mods, l, group_of_tile, tile_kind,
            head_gain[l].reshape(1, -1), pmats, w_pool[l].astype(BF16), pool_scale[l].reshape(1, -1),
            w_out[l].astype(BF16), norm_ffn[l].reshape(1, d), rw, rb)

        n_tok = n_lat if last else t_all
        idx_pad, gate_pad, count_pad = _route(logits, n_tok)
        n_assign = n_tok * TOP_K
        n_blocks = -(-n_assign // MOE_ROWS) + N_EXPERTS
        dest, block_e, block_rows, n_active = _routing_plan(
            idx_pad[:, :TOP_K], idx_pad[:, TOP_K:2 * TOP_K], count_pad[0, :N_EXPERTS], n_blocks)
        slots = _dispatch(dest, h2, n_tok, n_blocks * MOE_ROWS)
        y_slots = _moe(slots, block_e, block_rows, n_active, l, w_gate_up, b_gate_up, w_down, b_down)
        x_next = _combine(dest, gate_pad, x_mid, mods, l, group_of_tile_c, y_slots, n_tok,
                          norm_final.reshape(1, d), last)
        if last:
            out = x_next.reshape(b, s, d)
        else:
            x_all = x_next
    return out
```

```python
import functools

import jax
import jax.numpy as jnp
from jax import lax
from jax.experimental import pallas as pl
from jax.experimental.pallas import tpu as pltpu

F32 = jnp.float32
BF16 = jnp.bfloat16
U32 = jnp.uint32
I32 = jnp.int32

GRID_W = 64
N_HEADS = 4
HEAD_V = 256
HEAD_QK = 128
CHUNK = 128
POOL_WINDOWS = (2, 4, 8, 16)
N_POOL_GROUPS = 4
POOL_GROUP = 256
N_EXPERTS = 32
TOP_K = 4
SWIGLU_ALPHA = 1.702
SWIGLU_LIMIT = 7.0
N_MOD = 6
EPS = 1e-6
LANES = 128
MOD_ROWS = 8

SUBLANES = 8

ROW_TILE = 512
MOE_ROWS = 1024
MOE_FF_TILE = 512
DISPATCH_ROWS = 256
COMBINE_ROWS = 256
VMEM_LIMIT = 56 * 1024 * 1024
MOE_VMEM_LIMIT = 60 * 1024 * 1024


def _cparams(sem, vmem_limit=VMEM_LIMIT):
    return pltpu.CompilerParams(dimension_semantics=sem, vmem_limit_bytes=vmem_limit)


def _pack_bf16_pair(lo, hi):
    lo_bits = pltpu.bitcast(lo.astype(BF16).astype(F32), U32) >> 16
    hi_bits = pltpu.bitcast(hi.astype(BF16).astype(F32), U32) & jnp.uint32(0xFFFF0000)
    return hi_bits | lo_bits


def _unpack_bf16_pair(w):
    lo = pltpu.bitcast(w << 16, F32)
    hi = pltpu.bitcast(w & jnp.uint32(0xFFFF0000), F32)
    return lo, hi


def _tile_chunk(ref, c, n_rows):
    return ref[pl.ds(c, n_rows, stride=SUBLANES), :]


def _store_token_tiles(ref, read_cols, n_rows, d):
    half = d // 2
    assert half == SUBLANES * LANES
    for c in range(SUBLANES):
        lo = read_cols(slice(c * LANES, (c + 1) * LANES))
        hi = read_cols(slice(half + c * LANES, half + (c + 1) * LANES))
        ref[pl.ds(c, n_rows, stride=SUBLANES), :] = _pack_bf16_pair(lo, hi)


def _mod_kernel(c_ref, w_ref, b_ref, o_ref):
    c = c_ref[...]
    s = c * jax.nn.sigmoid(c)
    o_ref[...] = jnp.dot(s, w_ref[...], preferred_element_type=F32,
                         precision=lax.Precision.HIGHEST) + b_ref[...]


def _modulation(c_rows, w_mod, b_mod):
    depth, d, n = w_mod.shape
    tn = 1024
    return pl.pallas_call(
        _mod_kernel,
        grid=(depth, n // tn),
        in_specs=[
            pl.BlockSpec((MOD_ROWS, d), lambda l, j: (0, 0)),
            pl.BlockSpec((None, d, tn), lambda l, j: (l, 0, j)),
            pl.BlockSpec((None, 1, tn), lambda l, j: (l, 0, j)),
        ],
        out_specs=pl.BlockSpec((None, MOD_ROWS, tn), lambda l, j: (l, 0, j)),
        out_shape=jax.ShapeDtypeStruct((depth, MOD_ROWS, n), F32),
        compiler_params=_cparams(("arbitrary", "arbitrary")),
        name="modulation",
    )(c_rows, w_mod, b_mod.reshape(depth, 1, n))


def _in_proj_kernel(n_latent_tiles, *refs):
    if n_latent_tiles is None:
        x_ref, sh_ref, sc_ref, g_ref, w_ref, wg_ref, p_ref, gate_ref, h_scr = refs
    else:
        x_ref, ctx_ref, sh_ref, sc_ref, g_ref, w_ref, wg_ref, p_ref, gate_ref, xall_ref, h_scr = refs

    @pl.when(pl.program_id(1) == 0)
    def _():
        x = x_ref[...]
        if n_latent_tiles is not None:
            x = jnp.where(pl.program_id(0) >= n_latent_tiles, ctx_ref[...], x)
            xall_ref[...] = x
        y = x * lax.rsqrt(jnp.mean(x * x, axis=-1, keepdims=True) + EPS) * g_ref[...]
        h = (y * (1.0 + sc_ref[...]) + sh_ref[...]).astype(BF16)
        h_scr[...] = h
        gate_ref[...] = jnp.dot(h, wg_ref[...], preferred_element_type=F32)

    p_ref[...] = jnp.dot(h_scr[...], w_ref[...], preferred_element_type=F32).astype(BF16)


def _in_proj(x_rows, ctx_rows, mods, layer, group_of_tile, norm_g, w_main, w_gate):
    d = x_rows.shape[1]
    split = ctx_rows is not None
    n_latent_tiles = x_rows.shape[0] // ROW_TILE if split else None
    t_all = x_rows.shape[0] + (ctx_rows.shape[0] if split else 0)
    n_main = w_main.shape[1]
    tn = 2048
    mod_spec = lambda k: pl.BlockSpec(
        (None, None, 1, d), lambda i, j: (layer, group_of_tile(i), 0, k))
    if split:
        assert ctx_rows.shape[0] == ROW_TILE
        x_specs = [pl.BlockSpec((ROW_TILE, d), lambda i, j: (jnp.minimum(i, n_latent_tiles - 1), 0)),
                   pl.BlockSpec((ROW_TILE, d), lambda i, j: (0, 0))]
        operands = (x_rows, ctx_rows)
    else:
        x_specs = [pl.BlockSpec((ROW_TILE, d), lambda i, j: (i, 0))]
        operands = (x_rows,)
    out_specs = [
        pl.BlockSpec((ROW_TILE, tn), lambda i, j: (i, j)),
        pl.BlockSpec((ROW_TILE, LANES), lambda i, j: (i, 0)),
    ]
    out_shape = [
        jax.ShapeDtypeStruct((t_all, n_main), BF16),
        jax.ShapeDtypeStruct((t_all, LANES), F32),
    ]
    if split:
        out_specs.append(pl.BlockSpec((ROW_TILE, d), lambda i, j: (i, 0)))
        out_shape.append(jax.ShapeDtypeStruct((t_all, d), F32))
    return pl.pallas_call(
        functools.partial(_in_proj_kernel, n_latent_tiles),
        grid=(t_all // ROW_TILE, n_main // tn),
        in_specs=x_specs + [
            mod_spec(0), mod_spec(1),
            pl.BlockSpec((1, d), lambda i, j: (0, 0)),
            pl.BlockSpec((d, tn), lambda i, j: (0, j)),
            pl.BlockSpec((d, LANES), lambda i, j: (0, 0)),
        ],
        out_specs=out_specs,
        out_shape=out_shape,
        scratch_shapes=[pltpu.VMEM((ROW_TILE, d), BF16)],
        compiler_params=_cparams(("arbitrary", "arbitrary")),
        name="in_proj",
    )(*operands, mods, mods, norm_g, w_main, w_gate)


def _log_sigmoid(x):
    return jnp.minimum(x, 0.0) - jnp.log(1.0 + jnp.exp(-jnp.abs(x)))


def _mlstm_chain(q, k, v, a, a_t, cs, cs_t, col_i, col_f, reverse, c_ref, n_ref, m_ref):
    bc = cs[:, col_f:col_f + 1]
    br = cs_t[col_f:col_f + 1, :]
    ir = a_t[col_i:col_i + 1, :]
    ic = a[:, col_i:col_i + 1]
    t_idx = lax.broadcasted_iota(I32, (CHUNK, CHUNK), 0)
    s_idx = lax.broadcasted_iota(I32, (CHUNK, CHUNK), 1)
    mask = (s_idx >= t_idx) if reverse else (s_idx <= t_idx)
    dmat = jnp.where(mask, bc - br + ir, -jnp.inf)
    m_prev = m_ref[...]
    m_inter = bc + m_prev
    m_t = jnp.maximum(m_inter, jnp.max(dmat, axis=-1, keepdims=True))
    scale = HEAD_QK ** -0.5
    qk = lax.dot_general(q, k, (((1,), (1,)), ((), ())), preferred_element_type=F32)
    s = qk * (scale * jnp.exp(dmat - m_t))
    w_inter = scale * jnp.exp(m_inter - m_t)
    c_old = c_ref[...]
    num = (jnp.dot(s.astype(BF16), v, preferred_element_type=F32)
           + w_inter * jnp.dot(q, c_old.astype(BF16), preferred_element_type=F32))
    qn = jnp.sum(q.astype(F32) * n_ref[...], axis=-1, keepdims=True)
    den = jnp.sum(s, axis=-1, keepdims=True) + w_inter * qn
    h = num / jnp.maximum(jnp.abs(den), jnp.exp(-m_t))

    b_end = bc[0:1, :] if reverse else bc[CHUNK - 1:CHUNK, :]
    g = b_end - bc + ic
    m_new = jnp.maximum(b_end + m_prev, jnp.max(g, axis=0, keepdims=True))
    wg = jnp.exp(g - m_new)
    wc = jnp.exp(b_end + m_prev - m_new)
    kw = k.astype(F32) * wg
    c_ref[...] = wc * c_old + lax.dot_general(
        kw.astype(BF16), v, (((0,), (0,)), ((), ())), preferred_element_type=F32)
    n_ref[...] = wc * n_ref[...] + jnp.sum(kw, axis=0, keepdims=True)
    m_ref[...] = m_new
    return h


def _mlstm_kernel(batch, *refs):
    in_refs = refs[:8 * batch]
    bias_ref, hf_ref, hb_ref, c_scr, n_scr, m_scr = refs[8 * batch:]

    @pl.when(pl.program_id(0) == 0)
    def _():
        c_scr[...] = jnp.zeros_like(c_scr)
        n_scr[...] = jnp.zeros_like(n_scr)
        m_scr[...] = jnp.zeros_like(m_scr)

    lane = lax.broadcasted_iota(I32, (CHUNK, LANES), 1)
    is_f = ((lane >= N_HEADS) & (lane < 2 * N_HEADS)) | ((lane >= 3 * N_HEADS) & (lane < 4 * N_HEADS))
    t_idx = lax.broadcasted_iota(I32, (CHUNK, CHUNK), 0)
    s_idx = lax.broadcasted_iota(I32, (CHUNK, CHUNK), 1)

    for b in range(batch):
        for direction, out_ref in enumerate((hf_ref, hb_ref)):
            q_ref, k_ref, v_ref, g_ref = in_refs[8 * b + 4 * direction:8 * b + 4 * direction + 4]
            reverse = direction == 1
            graw = g_ref[...] + bias_ref[...]
            a = jnp.where(is_f, _log_sigmoid(graw), graw)
            tri = ((s_idx >= t_idx) if reverse else (s_idx <= t_idx)).astype(F32)
            cs = jnp.dot(tri, a, preferred_element_type=F32, precision=lax.Precision.HIGHEST)
            a_t = a.T
            cs_t = cs.T
            for hd in range(N_HEADS):
                col_i = 2 * direction * N_HEADS + hd
                col_f = col_i + N_HEADS
                h = _mlstm_chain(
                    q_ref[:, hd * HEAD_QK:(hd + 1) * HEAD_QK],
                    k_ref[:, hd * HEAD_QK:(hd + 1) * HEAD_QK],
                    v_ref[:, hd * HEAD_V:(hd + 1) * HEAD_V],
                    a, a_t, cs, cs_t, col_i, col_f, reverse,
                    c_scr.at[b, direction, hd], n_scr.at[b, direction, hd], m_scr.at[b, direction, hd])
                out_ref[b, :, hd * HEAD_V:(hd + 1) * HEAD_V] = h.astype(BF16)


def _mlstm(p_main, gates, bias_row, batch, seq, ctx_len):
    n_ctx = ctx_len // CHUNK
    n_seq = seq // CHUNK
    ctx_base = batch * seq // CHUNK
    d_qk = N_HEADS * HEAD_QK
    d_v = N_HEADS * HEAD_V

    def fwd_pos(i):
        return jnp.where(i < n_ctx, n_seq + i, i - n_ctx)

    def bwd_pos(i):
        return jnp.where(i < n_ctx, n_seq + (n_ctx - 1 - i), n_seq - 1 - (i - n_ctx))

    def row_blk(b, pos):
        return jnp.where(pos < n_seq, b * n_seq + pos, ctx_base + b * n_ctx + (pos - n_seq))

    def specs(b, pos_of):
        return [
            pl.BlockSpec((CHUNK, d_qk), lambda i: (row_blk(b, pos_of(i)), 0)),
            pl.BlockSpec((CHUNK, d_qk), lambda i: (row_blk(b, pos_of(i)), 1)),
            pl.BlockSpec((CHUNK, d_v), lambda i: (row_blk(b, pos_of(i)), 1)),
            pl.BlockSpec((CHUNK, LANES), lambda i: (row_blk(b, pos_of(i)), 0)),
        ]

    in_specs = []
    operands = []
    for b in range(batch):
        in_specs += specs(b, fwd_pos) + specs(b, bwd_pos)
        operands += [p_main, p_main, p_main, gates] * 2
    return pl.pallas_call(
        functools.partial(_mlstm_kernel, batch),
        grid=(n_ctx + n_seq,),
        in_specs=in_specs + [pl.BlockSpec((1, LANES), lambda i: (0, 0))],
        out_specs=[
            pl.BlockSpec((batch, CHUNK, d_v), lambda i: (0, fwd_pos(i), 0)),
            pl.BlockSpec((batch, CHUNK, d_v), lambda i: (0, bwd_pos(i), 0)),
        ],
        out_shape=[jax.ShapeDtypeStruct((batch, seq + ctx_len, d_v), BF16)] * 2,
        scratch_shapes=[
            pltpu.VMEM((batch, 2, N_HEADS, HEAD_QK, HEAD_V), F32),
            pltpu.VMEM((batch, 2, N_HEADS, 1, HEAD_QK), F32),
            pltpu.VMEM((batch, 2, N_HEADS, 1, 1), F32),
        ],
        compiler_params=_cparams(("arbitrary",)),
        name="mlstm",
    )(*operands, bias_row)


def _mix_out_kernel(n_latent_tiles, x_ref, o_ref, u_ref, hf_ref, hb_ref, hfc_ref, hbc_ref, gain_ref, pmat_ref,
                    wpool_ref, pscale_ref, wout_ref, g1_ref, sh2_ref, sc2_ref, nffn_ref, rw_ref, rb_ref,
                    xnew_ref, h2_ref, logit_ref, mix_scr):
    is_ctx = pl.program_id(0) >= n_latent_tiles
    n_rows = x_ref.shape[0]
    for hd in range(N_HEADS):
        sl = slice(hd * HEAD_V, (hd + 1) * HEAD_V)
        h_f = jnp.where(is_ctx, hfc_ref[:, :, sl].reshape(n_rows, HEAD_V), hf_ref[:, sl])
        h_b = jnp.where(is_ctx, hbc_ref[:, :, sl].reshape(n_rows, HEAD_V), hb_ref[:, sl])
        h = h_f.astype(F32) + h_b.astype(F32)
        hn = h * lax.rsqrt(jnp.mean(h * h, axis=-1, keepdims=True) + EPS) * gain_ref[:, sl]
        mix_scr[:, sl] = (jax.nn.sigmoid(o_ref[:, sl].astype(F32)) * hn).astype(BF16)
    d_mlstm = N_HEADS * HEAD_V
    for g in range(N_POOL_GROUPS):
        sl = slice(g * POOL_GROUP, (g + 1) * POOL_GROUP)
        u = u_ref[:, sl]
        dlt = jnp.dot(pmat_ref[g], u, preferred_element_type=F32) - u.astype(F32)
        y = jnp.dot(dlt.astype(BF16), wpool_ref[g], preferred_element_type=F32) * pscale_ref[:, sl]
        mix_scr[:, d_mlstm + g * POOL_GROUP:d_mlstm + (g + 1) * POOL_GROUP] = y.astype(BF16)
    acc = jnp.dot(mix_scr[...], wout_ref[...], preferred_element_type=F32)
    xn = x_ref[...] + g1_ref[...] * acc
    xnew_ref[...] = xn
    y2 = xn * lax.rsqrt(jnp.mean(xn * xn, axis=-1, keepdims=True) + EPS) * nffn_ref[...]
    h2 = y2 * (1.0 + sc2_ref[...]) + sh2_ref[...]
    _store_token_tiles(h2_ref, lambda sl: h2[:, sl], h2.shape[0], h2.shape[1])
    logit_ref[...] = jnp.dot(h2.astype(BF16), rw_ref[...], preferred_element_type=F32) + rb_ref[...]


def _mix_out(x_all, p_main, hf, hb, ctx_len, mods, layer, group_of_tile, tile_kind, gain, pmats, w_pool,
             pool_scale, w_out, norm_ffn, router_w, router_b):
    t_all, d = x_all.shape
    d_half = d // 2
    batch, positions, d_v = hf.shape
    seq = positions - ctx_len
    tiles_per_batch = seq // ROW_TILE
    n_latent_tiles = batch * tiles_per_batch
    assert batch * ctx_len == ROW_TILE and n_latent_tiles + 1 == t_all // ROW_TILE and seq % ctx_len == 0
    mod_spec = lambda k: pl.BlockSpec(
        (None, None, 1, d), lambda i: (layer, group_of_tile(i), 0, k))
    const = lambda shape: pl.BlockSpec(shape, lambda i: (0,) * len(shape), pipeline_mode=pl.Buffered(1))
    latent_spec = pl.BlockSpec(
        (None, ROW_TILE, d_v), lambda i: (jnp.minimum(i // tiles_per_batch, batch - 1), i % tiles_per_batch, 0))
    ctx_spec = pl.BlockSpec((batch, ctx_len, d_v), lambda i: (0, seq // ctx_len, 0), pipeline_mode=pl.Buffered(1))
    return pl.pallas_call(
        functools.partial(_mix_out_kernel, n_latent_tiles),
        grid=(t_all // ROW_TILE,),
        in_specs=[
            pl.BlockSpec((ROW_TILE, d), lambda i: (i, 0)),
            pl.BlockSpec((ROW_TILE, d_half), lambda i: (i, 2)),
            pl.BlockSpec((ROW_TILE, d_half), lambda i: (i, 3)),
            latent_spec, latent_spec, ctx_spec, ctx_spec,
            const((1, d_half)),
            pl.BlockSpec((None, N_POOL_GROUPS, ROW_TILE, ROW_TILE), lambda i: (tile_kind(i), 0, 0, 0)),
            const((N_POOL_GROUPS, POOL_GROUP, POOL_GROUP)),
            const((1, d_half)),
            const((d, d)),
            mod_spec(2), mod_spec(3), mod_spec(4),
            const((1, d)),
            const((d, LANES)),
            const((1, LANES)),
        ],
        out_specs=[
            pl.BlockSpec((ROW_TILE, d), lambda i: (i, 0)),
            pl.BlockSpec((ROW_TILE * SUBLANES, LANES), lambda i: (i, 0)),
            pl.BlockSpec((ROW_TILE, LANES), lambda i: (i, 0)),
        ],
        out_shape=[
            jax.ShapeDtypeStruct((t_all, d), F32),
            jax.ShapeDtypeStruct((t_all * SUBLANES, LANES), U32),
            jax.ShapeDtypeStruct((t_all, LANES), F32),
        ],
        scratch_shapes=[pltpu.VMEM((ROW_TILE, d), BF16)],
        compiler_params=_cparams(("arbitrary",)),
        name="mix_out",
    )(x_all, p_main, p_main, hf, hb, hf, hb, gain, pmats, w_pool, pool_scale, w_out,
      mods, mods, mods, norm_ffn, router_w, router_b)


def _route_kernel(logit_ref, idx_ref, gate_ref, count_ref, run_scr):
    @pl.when(pl.program_id(0) == 0)
    def _():
        run_scr[...] = jnp.zeros_like(run_scr)

    lg = logit_ref[...]
    n_rows = lg.shape[0]
    lane = lax.broadcasted_iota(I32, lg.shape, 1).astype(F32)
    lg = jnp.where(lane < N_EXPERTS, lg, -jnp.inf)
    vals, idxs, onehots = [], [], []
    for _ in range(TOP_K):
        mx = jnp.max(lg, axis=-1, keepdims=True)
        ix = jnp.min(jnp.where(lg == mx, lane, float(LANES)), axis=-1, keepdims=True)
        hit = lane == ix
        vals.append(mx)
        idxs.append(ix)
        onehots.append(hit.astype(F32))
        lg = jnp.where(hit, -jnp.inf, lg)
    es = [jnp.exp(v - vals[0]) for v in vals]
    tot = es[0] + es[1] + es[2] + es[3]

    onehot = onehots[0] + onehots[1] + onehots[2] + onehots[3]
    r_idx = lax.broadcasted_iota(I32, (n_rows, n_rows), 0)
    c_idx = lax.broadcasted_iota(I32, (n_rows, n_rows), 1)
    earlier = (c_idx < r_idx).astype(BF16)
    before = jnp.dot(earlier, onehot.astype(BF16), preferred_element_type=F32) + run_scr[...]
    run_scr[...] += jnp.sum(onehot, axis=0, keepdims=True)

    idx_out = jnp.zeros(lg.shape, F32)
    gate_out = jnp.zeros(lg.shape, F32)
    for k in range(TOP_K):
        rank = jnp.sum(onehots[k] * before, axis=-1, keepdims=True)
        idx_out = jnp.where(lane == k, idxs[k], idx_out)
        idx_out = jnp.where(lane == TOP_K + k, rank, idx_out)
        gate_out = jnp.where(lane == k, es[k] / tot, gate_out)
    idx_ref[...] = idx_out.astype(I32)
    gate_ref[...] = gate_out
    count_ref[...] = jnp.broadcast_to(run_scr[...], count_ref.shape).astype(I32)


def _route(logits, n_tokens):
    spec = pl.BlockSpec((ROW_TILE, LANES), lambda i: (i, 0))
    return pl.pallas_call(
        _route_kernel,
        grid=(n_tokens // ROW_TILE,),
        in_specs=[spec],
        out_specs=[spec, spec, pl.BlockSpec((SUBLANES, LANES), lambda i: (0, 0))],
        out_shape=[jax.ShapeDtypeStruct((n_tokens, LANES), I32), jax.ShapeDtypeStruct((n_tokens, LANES), F32),
                   jax.ShapeDtypeStruct((SUBLANES, LANES), I32)],
        scratch_shapes=[pltpu.VMEM((1, LANES), F32)],
        compiler_params=_cparams(("arbitrary",)),
        name="route",
    )(logits)


def _dispatch_kernel(dest_ref, h_ref, slots_hbm, sem):
    def tile_copy(t, k):
        src = pl.multiple_of(t * SUBLANES, SUBLANES)
        dst = pl.multiple_of(dest_ref[t * TOP_K + k] * SUBLANES, SUBLANES)
        return pltpu.make_async_copy(
            h_ref.at[pl.ds(src, SUBLANES)], slots_hbm.at[pl.ds(dst, SUBLANES)], sem)

    def issue(t, carry):
        for k in range(TOP_K):
            tile_copy(t, k).start(priority=k % 2)
        return carry

    lax.fori_loop(0, DISPATCH_ROWS, issue, 0, unroll=2)

    def drain(t, carry):
        for k in range(TOP_K):
            tile_copy(t, k).wait()
        return carry

    lax.fori_loop(0, DISPATCH_ROWS, drain, 0, unroll=2)


def _dispatch(dest_flat, h_tiles, n_tokens, n_slots):
    return pl.pallas_call(
        _dispatch_kernel,
        grid=(n_tokens // DISPATCH_ROWS,),
        in_specs=[
            pl.BlockSpec((DISPATCH_ROWS * TOP_K,), lambda i: (i,), memory_space=pltpu.SMEM),
            pl.BlockSpec((DISPATCH_ROWS * SUBLANES, LANES), lambda i: (i, 0)),
        ],
        out_specs=pl.BlockSpec(memory_space=pl.ANY),
        out_shape=jax.ShapeDtypeStruct((n_slots * SUBLANES, LANES), U32),
        scratch_shapes=[pltpu.SemaphoreType.DMA(())],
        compiler_params=_cparams(("arbitrary",)),
        name="dispatch",
    )(dest_flat, h_tiles)


def _moe_kernel(be_ref, nv_ref, na_ref, x_ref, wg_ref, wu_ref, bg_ref, bu_ref, wd_ref, bd_ref, o_ref,
                x_scr, acc_scr):
    i = pl.program_id(0)
    j = pl.program_id(1)
    n_rows, d = acc_scr.shape
    half = d // 2

    @pl.when(i < na_ref[0])
    def _():
        @pl.when(j == 0)
        def _():
            live = lax.broadcasted_iota(I32, (n_rows, LANES), 0) < nv_ref[i]
            for c in range(SUBLANES):
                words = jnp.where(live, _tile_chunk(x_ref, c, n_rows), jnp.uint32(0))
                lo, hi = _unpack_bf16_pair(words)
                x_scr[:, c * LANES:(c + 1) * LANES] = lo.astype(BF16)
                x_scr[:, half + c * LANES:half + (c + 1) * LANES] = hi.astype(BF16)

            @pl.when(i == 0)
            def _():
                acc_scr[...] = jnp.zeros_like(acc_scr)

        def expert_mlp(m):
            x = x_scr[:m, :]
            gl = jnp.dot(x, wg_ref[...].astype(BF16), preferred_element_type=F32) + bg_ref[...]
            ul = jnp.dot(x, wu_ref[...].astype(BF16), preferred_element_type=F32) + bu_ref[...]
            x_glu = jnp.minimum(gl, SWIGLU_LIMIT)
            x_lin = jnp.clip(ul, -SWIGLU_LIMIT, SWIGLU_LIMIT)
            act = x_glu * jax.nn.sigmoid(SWIGLU_ALPHA * x_glu) * (x_lin + 1.0)
            part = jnp.dot(act.astype(BF16), wd_ref[...].astype(BF16), preferred_element_type=F32)
            base = jnp.where(j == 0, jnp.broadcast_to(bd_ref[...], (m, d)), acc_scr[:m, :])
            acc_scr[:m, :] = base + part

        quarter = n_rows // 4
        for q in range(1, 5):
            @pl.when((nv_ref[i] > (q - 1) * quarter) & (nv_ref[i] <= q * quarter))
            def _(q=q):
                expert_mlp(q * quarter)

        @pl.when(j == pl.num_programs(1) - 1)
        def _():
            _store_token_tiles(o_ref, lambda sl: acc_scr[:, sl], n_rows, d)


def _moe(slots, block_e, block_rows, n_active, layer, w_gate_up, b_gate_up, w_down, b_down):
    n_slots = slots.shape[0] // SUBLANES
    d = w_down.shape[3]
    depth, n_e, _, two_f = w_gate_up.shape
    d_ff = two_f // 2
    n_j = d_ff // MOE_FF_TILE
    n_blocks = n_slots // MOE_ROWS

    def blk(i, na):
        return jnp.minimum(i, na[0] - 1)

    def jj(i, j, na):
        return jnp.where(i < na[0], j, n_j - 1)

    grid_spec = pltpu.PrefetchScalarGridSpec(
        num_scalar_prefetch=3,
        grid=(n_blocks, n_j),
        in_specs=[
            pl.BlockSpec((MOE_ROWS * SUBLANES, LANES), lambda i, j, be, nv, na: (blk(i, na), 0)),
            pl.BlockSpec((None, None, d, MOE_FF_TILE),
                         lambda i, j, be, nv, na: (layer, be[blk(i, na)], 0, jj(i, j, na))),
            pl.BlockSpec((None, None, d, MOE_FF_TILE),
                         lambda i, j, be, nv, na: (layer, be[blk(i, na)], 0, n_j + jj(i, j, na))),
            pl.BlockSpec((None, None, 1, MOE_FF_TILE),
                         lambda i, j, be, nv, na: (layer, be[blk(i, na)], 0, jj(i, j, na))),
            pl.BlockSpec((None, None, 1, MOE_FF_TILE),
                         lambda i, j, be, nv, na: (layer, be[blk(i, na)], 0, n_j + jj(i, j, na))),
            pl.BlockSpec((None, None, MOE_FF_TILE, d),
                         lambda i, j, be, nv, na: (layer, be[blk(i, na)], jj(i, j, na), 0)),
            pl.BlockSpec((None, None, 1, d), lambda i, j, be, nv, na: (layer, be[blk(i, na)], 0, 0)),
        ],
        out_specs=pl.BlockSpec((MOE_ROWS * SUBLANES, LANES), lambda i, j, be, nv, na: (blk(i, na), 0)),
        scratch_shapes=[pltpu.VMEM((MOE_ROWS, d), BF16), pltpu.VMEM((MOE_ROWS, d), F32)],
    )
    return pl.pallas_call(
        _moe_kernel,
        grid_spec=grid_spec,
        out_shape=jax.ShapeDtypeStruct((n_slots * SUBLANES, LANES), U32),
        compiler_params=_cparams(("arbitrary", "arbitrary"), MOE_VMEM_LIMIT),
        name="moe",
    )(block_e, block_rows, n_active, slots, w_gate_up, w_gate_up,
      b_gate_up.reshape(depth, n_e, 1, two_f), b_gate_up.reshape(depth, n_e, 1, two_f),
      w_down, b_down.reshape(depth, n_e, 1, d))


def _combine_kernel(final_norm, dest_ref, dest_next_ref, gate_ref, x_ref, g2_ref, nf_ref, y_hbm, o_ref,
                    rows_scr, sems):
    i = pl.program_id(0)
    slot = i % 2

    def tile_copy(d_ref, buf, t, k):
        src = pl.multiple_of(d_ref[t * TOP_K + k] * SUBLANES, SUBLANES)
        dst = pl.multiple_of(t * SUBLANES, SUBLANES)
        return pltpu.make_async_copy(
            y_hbm.at[pl.ds(src, SUBLANES)], rows_scr.at[buf, k, pl.ds(dst, SUBLANES)], sems.at[buf])

    def start_tile(d_ref, buf):
        def issue(t, carry):
            for k in range(TOP_K):
                tile_copy(d_ref, buf, t, k).start(priority=k % 2)
            return carry
        lax.fori_loop(0, COMBINE_ROWS, issue, 0, unroll=2)

    @pl.when(i == 0)
    def _():
        start_tile(dest_ref, 0)

    has_next = i + 1 < pl.num_programs(0)
    for buf in range(2):
        @pl.when(has_next & (slot == 1 - buf))
        def _(buf=buf):
            start_tile(dest_next_ref, buf)

    for buf in range(2):
        @pl.when(slot == buf)
        def _(buf=buf):
            def drain(t, carry):
                for k in range(TOP_K):
                    tile_copy(dest_ref, buf, t, k).wait()
                return carry
            lax.fori_loop(0, COMBINE_ROWS, drain, 0, unroll=2)

    rows = rows_scr.at[slot]
    d = x_ref.shape[1]
    half = d // 2
    gates = gate_ref[...]
    gate_cols = [gates[:, k:k + 1] for k in range(TOP_K)]
    sumsq = None
    for c in range(SUBLANES):
        f_lo = None
        f_hi = None
        for k in range(TOP_K):
            lo, hi = _unpack_bf16_pair(_tile_chunk(rows.at[k], c, COMBINE_ROWS))
            f_lo = gate_cols[k] * lo if f_lo is None else f_lo + gate_cols[k] * lo
            f_hi = gate_cols[k] * hi if f_hi is None else f_hi + gate_cols[k] * hi
        sl_lo = slice(c * LANES, (c + 1) * LANES)
        sl_hi = slice(half + c * LANES, half + (c + 1) * LANES)
        y_lo = x_ref[:, sl_lo] + g2_ref[:, sl_lo] * f_lo
        y_hi = x_ref[:, sl_hi] + g2_ref[:, sl_hi] * f_hi
        o_ref[:, sl_lo] = y_lo
        o_ref[:, sl_hi] = y_hi
        if final_norm:
            part = jnp.sum(y_lo * y_lo + y_hi * y_hi, axis=-1, keepdims=True)
            sumsq = part if sumsq is None else sumsq + part
    if final_norm:
        o_ref[...] = o_ref[...] * lax.rsqrt(sumsq / d + EPS) * nf_ref[...]


def _combine(dest_flat, gates, x_all, mods, layer, group_of_tile_c, y_slots, n_tokens, norm_final, final_norm):
    d = x_all.shape[1]
    n_steps = n_tokens // COMBINE_ROWS
    return pl.pallas_call(
        functools.partial(_combine_kernel, final_norm),
        grid=(n_steps,),
        in_specs=[
            pl.BlockSpec((COMBINE_ROWS * TOP_K,), lambda i: (i,), memory_space=pltpu.SMEM),
            pl.BlockSpec((COMBINE_ROWS * TOP_K,), lambda i: (jnp.minimum(i + 1, n_steps - 1),),
                         memory_space=pltpu.SMEM),
            pl.BlockSpec((COMBINE_ROWS, LANES), lambda i: (i, 0)),
            pl.BlockSpec((COMBINE_ROWS, d), lambda i: (i, 0)),
            pl.BlockSpec((None, None, 1, d), lambda i: (layer, group_of_tile_c(i), 0, 5)),
            pl.BlockSpec((1, d), lambda i: (0, 0)),
            pl.BlockSpec(memory_space=pl.ANY),
        ],
        out_specs=pl.BlockSpec((COMBINE_ROWS, d), lambda i: (i, 0)),
        out_shape=jax.ShapeDtypeStruct((n_tokens, d), F32),
        scratch_shapes=[pltpu.VMEM((2, TOP_K, COMBINE_ROWS * SUBLANES, LANES), U32),
                        pltpu.SemaphoreType.DMA((2,))],
        compiler_params=_cparams(("arbitrary",)),
        name="combine",
    )(dest_flat, dest_flat, gates, x_all, mods, norm_final, y_slots)


def _pool_matrices(seg_len, tile):
    pos = jnp.arange(tile)
    seg = pos // seg_len
    p = pos % seg_len
    mats = []
    for w in POOL_WINDOWS:
        left = w // 2
        right = w - 1 - left
        lo = jnp.clip(p - left, 0, seg_len)
        hi = jnp.clip(p + right + 1, 0, seg_len)
        inside = ((seg[:, None] == seg[None, :]) & (p[None, :] >= lo[:, None]) & (p[None, :] < hi[:, None]))
        mats.append(jnp.where(inside, 1.0 / (hi - lo).astype(F32)[:, None], 0.0))
    return jnp.stack(mats).astype(BF16)


def _routing_plan(top_idx, rank, counts, n_blocks):
    experts = jnp.arange(N_EXPERTS, dtype=I32)
    blocks_e = (counts + MOE_ROWS - 1) // MOE_ROWS
    blk_end = jnp.cumsum(blocks_e)
    blk_start = blk_end - blocks_e
    start_e = blk_start * MOE_ROWS
    start_of = jnp.sum(jnp.where(top_idx[..., None] == experts, start_e, 0), axis=-1)
    dest = (start_of + rank).reshape(-1).astype(I32)
    blk = jnp.arange(n_blocks, dtype=I32)
    block_e = jnp.minimum(jnp.sum((blk[:, None] >= blk_end[None, :]).astype(I32), axis=-1), N_EXPERTS - 1)
    in_e = block_e[:, None] == experts
    local = blk - jnp.sum(jnp.where(in_e, blk_start, 0), axis=-1)
    rows_e = jnp.sum(jnp.where(in_e, counts, 0), axis=-1)
    block_rows = jnp.clip(rows_e - local * MOE_ROWS, 0, MOE_ROWS).astype(I32)
    n_active = blk_end[-1:].astype(I32)
    return dest, block_e.astype(I32), block_rows, n_active


def kernel(x, c, ctx, c_ctx, w_mod, b_mod, norm_mix, w_in, b_gate, head_gain, w_pool, pool_scale,
           w_out, norm_ffn, router_w, router_b, w_gate_up, b_gate_up, w_down, b_down, norm_final):
    b, s, d = x.shape
    ctx_len = ctx.shape[1]
    depth = w_mod.shape[0]
    n_lat = b * s
    n_ctx_rows = b * ctx_len
    t_all = n_lat + n_ctx_rows
    d_main = w_in.shape[2] - 4 * N_HEADS
    assert s % ROW_TILE == 0 and n_ctx_rows % ROW_TILE == 0 and ROW_TILE % ctx_len == 0
    assert ROW_TILE % GRID_W == 0 and s % CHUNK == 0 and ctx_len % CHUNK == 0
    assert b + 1 <= MOD_ROWS

    x_all = None
    c_rows = jnp.zeros((MOD_ROWS, d), F32).at[:b].set(c).at[b].set(c_ctx)
    mods = _modulation(c_rows, w_mod, b_mod).reshape(depth, MOD_ROWS, 1, N_MOD * d)

    tiles_per_batch = s // ROW_TILE
    group_of_tile = lambda i: jnp.minimum(i // tiles_per_batch, b)
    tile_kind = lambda i: (i >= n_lat // ROW_TILE).astype(I32)
    group_of_tile_c = lambda i: jnp.minimum(i // (s // COMBINE_ROWS), b)
    pmats = jnp.stack([_pool_matrices(GRID_W, ROW_TILE), _pool_matrices(ctx_len, ROW_TILE)])

    out = None
    for l in range(depth):
        last = l == depth - 1
        w_main = w_in[l, :, :d_main].astype(BF16)
        w_gate = jnp.pad(w_in[l, :, d_main:], ((0, 0), (0, LANES - 4 * N_HEADS))).astype(BF16)
        bias_row = jnp.pad(b_gate[l].reshape(1, 4 * N_HEADS), ((0, 0), (0, LANES - 4 * N_HEADS)))
        if x_all is None:
            p_main, gates, x_all = _in_proj(x.reshape(n_lat, d), ctx.reshape(n_ctx_rows, d), mods, l,
                                            group_of_tile, norm_mix[l].reshape(1, d), w_main, w_gate)
        else:
            p_main, gates = _in_proj(x_all, None, mods, l, group_of_tile, norm_mix[l].reshape(1, d),
                                     w_main, w_gate)
        hf, hb = _mlstm(p_main, gates, bias_row, b, s, ctx_len)
        rw = jnp.pad(router_w[l], ((0, 0), (0, LANES - N_EXPERTS))).astype(BF16)
        rb = jnp.pad(router_b[l].reshape(1, N_EXPERTS), ((0, 0), (0, LANES - N_EXPERTS)))
        x_mid, h2, logits = _mix_out(
            x_all, p_main, hf, hb, ctx_len, mods, l, group_of_tile, tile_kind,
            head_gain[l].reshape(1, -1), pmats, w_pool[l].astype(BF16), pool_scale[l].reshape(1, -1),
            w_out[l].astype(BF16), norm_ffn[l].reshape(1, d), rw, rb)

        n_tok = n_lat if last else t_all
        idx_pad, gate_pad, count_pad = _route(logits, n_tok)
        n_assign = n_tok * TOP_K
        n_blocks = -(-n_assign // MOE_ROWS) + N_EXPERTS
        dest, block_e, block_rows, n_active = _routing_plan(
            idx_pad[:, :TOP_K], idx_pad[:, TOP_K:2 * TOP_K], count_pad[0, :N_EXPERTS], n_blocks)
        slots = _dispatch(dest, h2, n_tok, n_blocks * MOE_ROWS)
        y_slots = _moe(slots, block_e, block_rows, n_active, l, w_gate_up, b_gate_up, w_down, b_down)
        x_next = _combine(dest, gate_pad, x_mid, mods, l, group_of_tile_c, y_slots, n_tok,
                          norm_final.reshape(1, d), last)
        if last:
            out = x_next.reshape(b, s, d)
        else:
            x_all = x_next
    return out
```

```python
import functools

import jax
import jax.numpy as jnp
from jax import lax
from jax.experimental import pallas as pl
from jax.experimental.pallas import tpu as pltpu

F32 = jnp.float32
BF16 = jnp.bfloat16
U32 = jnp.uint32
I32 = jnp.int32

GRID_W = 64
N_HEADS = 4
HEAD_V = 256
HEAD_QK = 128
CHUNK = 128
POOL_WINDOWS = (2, 4, 8, 16)
N_POOL_GROUPS = 4
POOL_GROUP = 256
N_EXPERTS = 32
TOP_K = 4
SWIGLU_ALPHA = 1.702
SWIGLU_LIMIT = 7.0
N_MOD = 6
EPS = 1e-6
LANES = 128
MOD_ROWS = 8

SUBLANES = 8

ROW_TILE = 512
MOE_ROWS = 1024
MOE_FF_TILE = 512
DISPATCH_ROWS = 256
COMBINE_ROWS = 256
VMEM_LIMIT = 56 * 1024 * 1024
MOE_VMEM_LIMIT = 60 * 1024 * 1024


def _cparams(sem, vmem_limit=VMEM_LIMIT):
    return pltpu.CompilerParams(dimension_semantics=sem, vmem_limit_bytes=vmem_limit)


def _pack_bf16_pair(lo, hi):
    lo_bits = pltpu.bitcast(lo.astype(BF16).astype(F32), U32) >> 16
    hi_bits = pltpu.bitcast(hi.astype(BF16).astype(F32), U32) & jnp.uint32(0xFFFF0000)
    return hi_bits | lo_bits


def _unpack_bf16_pair(w):
    lo = pltpu.bitcast(w << 16, F32)
    hi = pltpu.bitcast(w & jnp.uint32(0xFFFF0000), F32)
    return lo, hi


def _tile_chunk(ref, c, n_rows):
    return ref[pl.ds(c, n_rows, stride=SUBLANES), :]


def _store_token_tiles(ref, read_cols, n_rows, d):
    half = d // 2
    assert half == SUBLANES * LANES
    for c in range(SUBLANES):
        lo = read_cols(slice(c * LANES, (c + 1) * LANES))
        hi = read_cols(slice(half + c * LANES, half + (c + 1) * LANES))
        ref[pl.ds(c, n_rows, stride=SUBLANES), :] = _pack_bf16_pair(lo, hi)


def _mod_kernel(c_ref, w_ref, b_ref, o_ref):
    c = c_ref[...]
    s = c * jax.nn.sigmoid(c)
    o_ref[...] = jnp.dot(s, w_ref[...], preferred_element_type=F32,
                         precision=lax.Precision.HIGHEST) + b_ref[...]


def _modulation(c_rows, w_mod, b_mod):
    depth, d, n = w_mod.shape
    tn = 1024
    return pl.pallas_call(
        _mod_kernel,
        grid=(depth, n // tn),
        in_specs=[
            pl.BlockSpec((MOD_ROWS, d), lambda l, j: (0, 0)),
            pl.BlockSpec((None, d, tn), lambda l, j: (l, 0, j)),
            pl.BlockSpec((None, 1, tn), lambda l, j: (l, 0, j)),
        ],
        out_specs=pl.BlockSpec((None, MOD_ROWS, tn), lambda l, j: (l, 0, j)),
        out_shape=jax.ShapeDtypeStruct((depth, MOD_ROWS, n), F32),
        compiler_params=_cparams(("arbitrary", "arbitrary")),
        name="modulation",
    )(c_rows, w_mod, b_mod.reshape(depth, 1, n))


def _in_proj_kernel(n_latent_tiles, *refs):
    if n_latent_tiles is None:
        x_ref, sh_ref, sc_ref, g_ref, w_ref, wg_ref, p_ref, gate_ref, h_scr = refs
    else:
        x_ref, ctx_ref, sh_ref, sc_ref, g_ref, w_ref, wg_ref, p_ref, gate_ref, xall_ref, h_scr = refs

    @pl.when(pl.program_id(1) == 0)
    def _():
        x = x_ref[...]
        if n_latent_tiles is not None:
            x = jnp.where(pl.program_id(0) >= n_latent_tiles, ctx_ref[...], x)
            xall_ref[...] = x
        y = x * lax.rsqrt(jnp.mean(x * x, axis=-1, keepdims=True) + EPS) * g_ref[...]
        h = (y * (1.0 + sc_ref[...]) + sh_ref[...]).astype(BF16)
        h_scr[...] = h
        gate_ref[...] = jnp.dot(h, wg_ref[...], preferred_element_type=F32)

    p_ref[...] = jnp.dot(h_scr[...], w_ref[...], preferred_element_type=F32).astype(BF16)


def _in_proj(x_rows, ctx_rows, mods, layer, group_of_tile, norm_g, w_main, w_gate):
    d = x_rows.shape[1]
    split = ctx_rows is not None
    n_latent_tiles = x_rows.shape[0] // ROW_TILE if split else None
    t_all = x_rows.shape[0] + (ctx_rows.shape[0] if split else 0)
    n_main = w_main.shape[1]
    tn = 2048
    mod_spec = lambda k: pl.BlockSpec(
        (None, None, 1, d), lambda i, j: (layer, group_of_tile(i), 0, k))
    if split:
        assert ctx_rows.shape[0] == ROW_TILE
        x_specs = [pl.BlockSpec((ROW_TILE, d), lambda i, j: (jnp.minimum(i, n_latent_tiles - 1), 0)),
                   pl.BlockSpec((ROW_TILE, d), lambda i, j: (0, 0))]
        operands = (x_rows, ctx_rows)
    else:
        x_specs = [pl.BlockSpec((ROW_TILE, d), lambda i, j: (i, 0))]
        operands = (x_rows,)
    out_specs = [
        pl.BlockSpec((ROW_TILE, tn), lambda i, j: (i, j)),
        pl.BlockSpec((ROW_TILE, LANES), lambda i, j: (i, 0)),
    ]
    out_shape = [
        jax.ShapeDtypeStruct((t_all, n_main), BF16),
        jax.ShapeDtypeStruct((t_all, LANES), F32),
    ]
    if split:
        out_specs.append(pl.BlockSpec((ROW_TILE, d), lambda i, j: (i, 0)))
        out_shape.append(jax.ShapeDtypeStruct((t_all, d), F32))
    return pl.pallas_call(
        functools.partial(_in_proj_kernel, n_latent_tiles),
        grid=(t_all // ROW_TILE, n_main // tn),
        in_specs=x_specs + [
            mod_spec(0), mod_spec(1),
            pl.BlockSpec((1, d), lambda i, j: (0, 0)),
            pl.BlockSpec((d, tn), lambda i, j: (0, j)),
            pl.BlockSpec((d, LANES), lambda i, j: (0, 0)),
        ],
        out_specs=out_specs,
        out_shape=out_shape,
        scratch_shapes=[pltpu.VMEM((ROW_TILE, d), BF16)],
        compiler_params=_cparams(("arbitrary", "arbitrary")),
        name="in_proj",
    )(*operands, mods, mods, norm_g, w_main, w_gate)


def _log_sigmoid(x):
    return jnp.minimum(x, 0.0) - jnp.log(1.0 + jnp.exp(-jnp.abs(x)))


def _mlstm_chain(q, k, v, a, a_t, cs, cs_t, col_i, col_f, reverse, c_ref, n_ref, m_ref):
    bc = cs[:, col_f:col_f + 1]
    br = cs_t[col_f:col_f + 1, :]
    ir = a_t[col_i:col_i + 1, :]
    ic = a[:, col_i:col_i + 1]
    t_idx = lax.broadcasted_iota(I32, (CHUNK, CHUNK), 0)
    s_idx = lax.broadcasted_iota(I32, (CHUNK, CHUNK), 1)
    mask = (s_idx >= t_idx) if reverse else (s_idx <= t_idx)
    dmat = jnp.where(mask, bc - br + ir, -jnp.inf)
    m_prev = m_ref[...]
    m_inter = bc + m_prev
    m_t = jnp.maximum(m_inter, jnp.max(dmat, axis=-1, keepdims=True))
    scale = HEAD_QK ** -0.5
    qk = lax.dot_general(q, k, (((1,), (1,)), ((), ())), preferred_element_type=F32)
    s = qk * (scale * jnp.exp(dmat - m_t))
    w_inter = scale * jnp.exp(m_inter - m_t)
    c_old = c_ref[...]
    num = (jnp.dot(s.astype(BF16), v, preferred_element_type=F32)
           + w_inter * jnp.dot(q, c_old.astype(BF16), preferred_element_type=F32))
    qn = jnp.sum(q.astype(F32) * n_ref[...], axis=-1, keepdims=True)
    den = jnp.sum(s, axis=-1, keepdims=True) + w_inter * qn
    h = num / jnp.maximum(jnp.abs(den), jnp.exp(-m_t))

    b_end = bc[0:1, :] if reverse else bc[CHUNK - 1:CHUNK, :]
    g = b_end - bc + ic
    m_new = jnp.maximum(b_end + m_prev, jnp.max(g, axis=0, keepdims=True))
    wg = jnp.exp(g - m_new)
    wc = jnp.exp(b_end + m_prev - m_new)
    kw = k.astype(F32) * wg
    c_ref[...] = wc * c_old + lax.dot_general(
        kw.astype(BF16), v, (((0,), (0,)), ((), ())), preferred_element_type=F32)
    n_ref[...] = wc * n_ref[...] + jnp.sum(kw, axis=0, keepdims=True)
    m_ref[...] = m_new
    return h


def _mlstm_kernel(batch, *refs):
    in_refs = refs[:8 * batch]
    bias_ref, hf_ref, hb_ref, c_scr, n_scr, m_scr = refs[8 * batch:]

    @pl.when(pl.program_id(0) == 0)
    def _():
        c_scr[...] = jnp.zeros_like(c_scr)
        n_scr[...] = jnp.zeros_like(n_scr)
        m_scr[...] = jnp.zeros_like(m_scr)

    lane = lax.broadcasted_iota(I32, (CHUNK, LANES), 1)
    is_f = ((lane >= N_HEADS) & (lane < 2 * N_HEADS)) | ((lane >= 3 * N_HEADS) & (lane < 4 * N_HEADS))
    t_idx = lax.broadcasted_iota(I32, (CHUNK, CHUNK), 0)
    s_idx = lax.broadcasted_iota(I32, (CHUNK, CHUNK), 1)

    for b in range(batch):
        for direction, out_ref in enumerate((hf_ref, hb_ref)):
            q_ref, k_ref, v_ref, g_ref = in_refs[8 * b + 4 * direction:8 * b + 4 * direction + 4]
            reverse = direction == 1
            graw = g_ref[...] + bias_ref[...]
            a = jnp.where(is_f, _log_sigmoid(graw), graw)
            tri = ((s_idx >= t_idx) if reverse else (s_idx <= t_idx)).astype(F32)
            cs = jnp.dot(tri, a, preferred_element_type=F32, precision=lax.Precision.HIGHEST)
            a_t = a.T
            cs_t = cs.T
            for hd in range(N_HEADS):
                col_i = 2 * direction * N_HEADS + hd
                col_f = col_i + N_HEADS
                h = _mlstm_chain(
                    q_ref[:, hd * HEAD_QK:(hd + 1) * HEAD_QK],
                    k_ref[:, hd * HEAD_QK:(hd + 1) * HEAD_QK],
                    v_ref[:, hd * HEAD_V:(hd + 1) * HEAD_V],
                    a, a_t, cs, cs_t, col_i, col_f, reverse,
                    c_scr.at[b, direction, hd], n_scr.at[b, direction, hd], m_scr.at[b, direction, hd])
                out_ref[b, :, hd * HEAD_V:(hd + 1) * HEAD_V] = h.astype(BF16)


def _mlstm(p_main, gates, bias_row, batch, seq, ctx_len):
    n_ctx = ctx_len // CHUNK
    n_seq = seq // CHUNK
    ctx_base = batch * seq // CHUNK
    d_qk = N_HEADS * HEAD_QK
    d_v = N_HEADS * HEAD_V

    def fwd_pos(i):
        return jnp.where(i < n_ctx, n_seq + i, i - n_ctx)

    def bwd_pos(i):
        return jnp.where(i < n_ctx, n_seq + (n_ctx - 1 - i), n_seq - 1 - (i - n_ctx))

    def row_blk(b, pos):
        return jnp.where(pos < n_seq, b * n_seq + pos, ctx_base + b * n_ctx + (pos - n_seq))

    def specs(b, pos_of):
        return [
            pl.BlockSpec((CHUNK, d_qk), lambda i: (row_blk(b, pos_of(i)), 0)),
            pl.BlockSpec((CHUNK, d_qk), lambda i: (row_blk(b, pos_of(i)), 1)),
            pl.BlockSpec((CHUNK, d_v), lambda i: (row_blk(b, pos_of(i)), 1)),
            pl.BlockSpec((CHUNK, LANES), lambda i: (row_blk(b, pos_of(i)), 0)),
        ]

    in_specs = []
    operands = []
    for b in range(batch):
        in_specs += specs(b, fwd_pos) + specs(b, bwd_pos)
        operands += [p_main, p_main, p_main, gates] * 2
    return pl.pallas_call(
        functools.partial(_mlstm_kernel, batch),
        grid=(n_ctx + n_seq,),
        in_specs=in_specs + [pl.BlockSpec((1, LANES), lambda i: (0, 0))],
        out_specs=[
            pl.BlockSpec((batch, CHUNK, d_v), lambda i: (0, fwd_pos(i), 0)),
            pl.BlockSpec((batch, CHUNK, d_v), lambda i: (0, bwd_pos(i), 0)),
        ],
        out_shape=[jax.ShapeDtypeStruct((batch, seq + ctx_len, d_v), BF16)] * 2,
        scratch_shapes=[
            pltpu.VMEM((batch, 2, N_HEADS, HEAD_QK, HEAD_V), F32),
            pltpu.VMEM((batch, 2, N_HEADS, 1, HEAD_QK), F32),
            pltpu.VMEM((batch, 2, N_HEADS, 1, 1), F32),
        ],
        compiler_params=_cparams(("arbitrary",)),
        name="mlstm",
    )(*operands, bias_row)


def _mix_out_kernel(n_latent_tiles, x_ref, o_ref, u_ref, hf_ref, hb_ref, hfc_ref, hbc_ref, gain_ref, pmat_ref,
                    wpool_ref, pscale_ref, wout_ref, g1_ref, sh2_ref, sc2_ref, nffn_ref, rw_ref, rb_ref,
                    xnew_ref, h2_ref, logit_ref, mix_scr):
    is_ctx = pl.program_id(0) >= n_latent_tiles
    n_rows = x_ref.shape[0]
    for hd in range(N_HEADS):
        sl = slice(hd * HEAD_V, (hd + 1) * HEAD_V)
        h_f = jnp.where(is_ctx, hfc_ref[:, :, sl].reshape(n_rows, HEAD_V), hf_ref[:, sl])
        h_b = jnp.where(is_ctx, hbc_ref[:, :, sl].reshape(n_rows, HEAD_V), hb_ref[:, sl])
        h = h_f.astype(F32) + h_b.astype(F32)
        hn = h * lax.rsqrt(jnp.mean(h * h, axis=-1, keepdims=True) + EPS) * gain_ref[:, sl]
        mix_scr[:, sl] = (jax.nn.sigmoid(o_ref[:, sl].astype(F32)) * hn).astype(BF16)
    d_mlstm = N_HEADS * HEAD_V
    for g in range(N_POOL_GROUPS):
        sl = slice(g * POOL_GROUP, (g + 1) * POOL_GROUP)
        u = u_ref[:, sl]
        dlt = jnp.dot(pmat_ref[g], u, preferred_element_type=F32) - u.astype(F32)
        y = jnp.dot(dlt.astype(BF16), wpool_ref[g], preferred_element_type=F32) * pscale_ref[:, sl]
        mix_scr[:, d_mlstm + g * POOL_GROUP:d_mlstm + (g + 1) * POOL_GROUP] = y.astype(BF16)
    acc = jnp.dot(mix_scr[...], wout_ref[...], preferred_element_type=F32)
    xn = x_ref[...] + g1_ref[...] * acc
    xnew_ref[...] = xn
    y2 = xn * lax.rsqrt(jnp.mean(xn * xn, axis=-1, keepdims=True) + EPS) * nffn_ref[...]
    h2 = y2 * (1.0 + sc2_ref[...]) + sh2_ref[...]
    _store_token_tiles(h2_ref, lambda sl: h2[:, sl], h2.shape[0], h2.shape[1])
    logit_ref[...] = jnp.dot(h2.astype(BF16), rw_ref[...], preferred_element_type=F32) + rb_ref[...]


def _mix_out(x_all, p_main, hf, hb, ctx_len, mods, layer, group_of_tile, tile_kind, gain, pmats, w_pool,
             pool_scale, w_out, norm_ffn, router_w, router_b):
    t_all, d = x_all.shape
    d_half = d // 2
    batch, positions, d_v = hf.shape
    seq = positions - ctx_len
    tiles_per_batch = seq // ROW_TILE
    n_latent_tiles = batch * tiles_per_batch
    assert batch * ctx_len == ROW_TILE and n_latent_tiles + 1 == t_all // ROW_TILE and seq % ctx_len == 0
    mod_spec = lambda k: pl.BlockSpec(
        (None, None, 1, d), lambda i: (layer, group_of_tile(i), 0, k))
    const = lambda shape: pl.BlockSpec(shape, lambda i: (0,) * len(shape), pipeline_mode=pl.Buffered(1))
    latent_spec = pl.BlockSpec(
        (None, ROW_TILE, d_v), lambda i: (jnp.minimum(i // tiles_per_batch, batch - 1), i % tiles_per_batch, 0))
    ctx_spec = pl.BlockSpec((batch, ctx_len, d_v), lambda i: (0, seq // ctx_len, 0), pipeline_mode=pl.Buffered(1))
    return pl.pallas_call(
        functools.partial(_mix_out_kernel, n_latent_tiles),
        grid=(t_all // ROW_TILE,),
        in_specs=[
            pl.BlockSpec((ROW_TILE, d), lambda i: (i, 0)),
            pl.BlockSpec((ROW_TILE, d_half), lambda i: (i, 2)),
            pl.BlockSpec((ROW_TILE, d_half), lambda i: (i, 3)),
            latent_spec, latent_spec, ctx_spec, ctx_spec,
            const((1, d_half)),
            pl.BlockSpec((None, N_POOL_GROUPS, ROW_TILE, ROW_TILE), lambda i: (tile_kind(i), 0, 0, 0)),
            const((N_POOL_GROUPS, POOL_GROUP, POOL_GROUP)),
            const((1, d_half)),
            const((d, d)),
            mod_spec(2), mod_spec(3), mod_spec(4),
            const((1, d)),
            const((d, LANES)),
            const((1, LANES)),
        ],
        out_specs=[
            pl.BlockSpec((ROW_TILE, d), lambda i: (i, 0)),
            pl.BlockSpec((ROW_TILE * SUBLANES, LANES), lambda i: (i, 0)),
            pl.BlockSpec((ROW_TILE, LANES), lambda i: (i, 0)),
        ],
        out_shape=[
            jax.ShapeDtypeStruct((t_all, d), F32),
            jax.ShapeDtypeStruct((t_all * SUBLANES, LANES), U32),
            jax.ShapeDtypeStruct((t_all, LANES), F32),
        ],
        scratch_shapes=[pltpu.VMEM((ROW_TILE, d), BF16)],
        compiler_params=_cparams(("arbitrary",)),
        name="mix_out",
    )(x_all, p_main, p_main, hf, hb, hf, hb, gain, pmats, w_pool, pool_scale, w_out,
      mods, mods, mods, norm_ffn, router_w, router_b)


def _route_kernel(logit_ref, idx_ref, gate_ref, count_ref, run_scr):
    @pl.when(pl.program_id(0) == 0)
    def _():
        run_scr[...] = jnp.zeros_like(run_scr)

    lg = logit_ref[...]
    n_rows = lg.shape[0]
    lane = lax.broadcasted_iota(I32, lg.shape, 1).astype(F32)
    lg = jnp.where(lane < N_EXPERTS, lg, -jnp.inf)
    vals, idxs, onehots = [], [], []
    for _ in range(TOP_K):
        mx = jnp.max(lg, axis=-1, keepdims=True)
        ix = jnp.min(jnp.where(lg == mx, lane, float(LANES)), axis=-1, keepdims=True)
        hit = lane == ix
        vals.append(mx)
        idxs.append(ix)
        onehots.append(hit.astype(F32))
        lg = jnp.where(hit, -jnp.inf, lg)
    es = [jnp.exp(v - vals[0]) for v in vals]
    tot = es[0] + es[1] + es[2] + es[3]

    onehot = onehots[0] + onehots[1] + onehots[2] + onehots[3]
    r_idx = lax.broadcasted_iota(I32, (n_rows, n_rows), 0)
    c_idx = lax.broadcasted_iota(I32, (n_rows, n_rows), 1)
    earlier = (c_idx < r_idx).astype(BF16)
    before = jnp.dot(earlier, onehot.astype(BF16), preferred_element_type=F32) + run_scr[...]
    run_scr[...] += jnp.sum(onehot, axis=0, keepdims=True)

    idx_out = jnp.zeros(lg.shape, F32)
    gate_out = jnp.zeros(lg.shape, F32)
    for k in range(TOP_K):
        rank = jnp.sum(onehots[k] * before, axis=-1, keepdims=True)
        idx_out = jnp.where(lane == k, idxs[k], idx_out)
        idx_out = jnp.where(lane == TOP_K + k, rank, idx_out)
        gate_out = jnp.where(lane == k, es[k] / tot, gate_out)
    idx_ref[...] = idx_out.astype(I32)
    gate_ref[...] = gate_out
    count_ref[...] = jnp.broadcast_to(run_scr[...], count_ref.shape).astype(I32)


def _route(logits, n_tokens):
    spec = pl.BlockSpec((ROW_TILE, LANES), lambda i: (i, 0))
    return pl.pallas_call(
        _route_kernel,
        grid=(n_tokens // ROW_TILE,),
        in_specs=[spec],
        out_specs=[spec, spec, pl.BlockSpec((SUBLANES, LANES), lambda i: (0, 0))],
        out_shape=[jax.ShapeDtypeStruct((n_tokens, LANES), I32), jax.ShapeDtypeStruct((n_tokens, LANES), F32),
                   jax.ShapeDtypeStruct((SUBLANES, LANES), I32)],
        scratch_shapes=[pltpu.VMEM((1, LANES), F32)],
        compiler_params=_cparams(("arbitrary",)),
        name="route",
    )(logits)


def _dispatch_kernel(dest_ref, h_ref, slots_hbm, sem):
    def tile_copy(t, k):
        src = pl.multiple_of(t * SUBLANES, SUBLANES)
        dst = pl.multiple_of(dest_ref[t * TOP_K + k] * SUBLANES, SUBLANES)
        return pltpu.make_async_copy(
            h_ref.at[pl.ds(src, SUBLANES)], slots_hbm.at[pl.ds(dst, SUBLANES)], sem)

    def issue(t, carry):
        for k in range(TOP_K):
            tile_copy(t, k).start(priority=k % 2)
        return carry

    lax.fori_loop(0, DISPATCH_ROWS, issue, 0, unroll=2)

    def drain(t, carry):
        for k in range(TOP_K):
            tile_copy(t, k).wait()
        return carry

    lax.fori_loop(0, DISPATCH_ROWS, drain, 0, unroll=2)


def _dispatch(dest_flat, h_tiles, n_tokens, n_slots):
    return pl.pallas_call(
        _dispatch_kernel,
        grid=(n_tokens // DISPATCH_ROWS,),
        in_specs=[
            pl.BlockSpec((DISPATCH_ROWS * TOP_K,), lambda i: (i,), memory_space=pltpu.SMEM),
            pl.BlockSpec((DISPATCH_ROWS * SUBLANES, LANES), lambda i: (i, 0)),
        ],
        out_specs=pl.BlockSpec(memory_space=pl.ANY),
        out_shape=jax.ShapeDtypeStruct((n_slots * SUBLANES, LANES), U32),
        scratch_shapes=[pltpu.SemaphoreType.DMA(())],
        compiler_params=_cparams(("arbitrary",)),
        name="dispatch",
    )(dest_flat, h_tiles)


def _moe_kernel(be_ref, nv_ref, na_ref, x_ref, wg_ref, wu_ref, bgu_ref, wd_ref, bd_ref, o_ref,
                x_scr, acc_scr):
    i = pl.program_id(0)
    j = pl.program_id(1)
    n_rows, d = acc_scr.shape
    half = d // 2
    bg = bgu_ref[j]
    bu = bgu_ref[pl.num_programs(1) + j]

    @pl.when(i < na_ref[0])
    def _():
        @pl.when(j == 0)
        def _():
            live = lax.broadcasted_iota(I32, (n_rows, LANES), 0) < nv_ref[i]
            for c in range(SUBLANES):
                words = jnp.where(live, _tile_chunk(x_ref, c, n_rows), jnp.uint32(0))
                lo, hi = _unpack_bf16_pair(words)
                x_scr[:, c * LANES:(c + 1) * LANES] = lo.astype(BF16)
                x_scr[:, half + c * LANES:half + (c + 1) * LANES] = hi.astype(BF16)

            @pl.when(i == 0)
            def _():
                acc_scr[...] = jnp.zeros_like(acc_scr)

        def expert_mlp(m):
            x = x_scr[:m, :]
            gl = jnp.dot(x, wg_ref[...].astype(BF16), preferred_element_type=F32) + bg
            ul = jnp.dot(x, wu_ref[...].astype(BF16), preferred_element_type=F32) + bu
            x_glu = jnp.minimum(gl, SWIGLU_LIMIT)
            x_lin = jnp.clip(ul, -SWIGLU_LIMIT, SWIGLU_LIMIT)
            act = x_glu * jax.nn.sigmoid(SWIGLU_ALPHA * x_glu) * (x_lin + 1.0)
            part = jnp.dot(act.astype(BF16), wd_ref[...].astype(BF16), preferred_element_type=F32)
            base = jnp.where(j == 0, jnp.broadcast_to(bd_ref[...], (m, d)), acc_scr[:m, :])
            acc_scr[:m, :] = base + part

        quarter = n_rows // 4
        for q in range(1, 5):
            @pl.when((nv_ref[i] > (q - 1) * quarter) & (nv_ref[i] <= q * quarter))
            def _(q=q):
                expert_mlp(q * quarter)

        @pl.when(j == pl.num_programs(1) - 1)
        def _():
            _store_token_tiles(o_ref, lambda sl: acc_scr[:, sl], n_rows, d)


def _moe(slots, block_e, block_rows, n_active, layer, w_gate_up, b_gate_up, w_down, b_down):
    n_slots = slots.shape[0] // SUBLANES
    d = w_down.shape[3]
    depth, n_e, _, two_f = w_gate_up.shape
    d_ff = two_f // 2
    n_j = d_ff // MOE_FF_TILE
    n_blocks = n_slots // MOE_ROWS

    def blk(i, na):
        return jnp.minimum(i, na[0] - 1)

    def jj(i, j, na):
        return jnp.where(i < na[0], j, n_j - 1)

    grid_spec = pltpu.PrefetchScalarGridSpec(
        num_scalar_prefetch=3,
        grid=(n_blocks, n_j),
        in_specs=[
            pl.BlockSpec((MOE_ROWS * SUBLANES, LANES), lambda i, j, be, nv, na: (blk(i, na), 0)),
            pl.BlockSpec((None, None, d, MOE_FF_TILE),
                         lambda i, j, be, nv, na: (layer, be[blk(i, na)], 0, jj(i, j, na))),
            pl.BlockSpec((None, None, d, MOE_FF_TILE),
                         lambda i, j, be, nv, na: (layer, be[blk(i, na)], 0, n_j + jj(i, j, na))),
            pl.BlockSpec((None, None, 2 * n_j, 1, MOE_FF_TILE),
                         lambda i, j, be, nv, na: (layer, be[blk(i, na)], 0, 0, 0)),
            pl.BlockSpec((None, None, MOE_FF_TILE, d),
                         lambda i, j, be, nv, na: (layer, be[blk(i, na)], jj(i, j, na), 0)),
            pl.BlockSpec((None, None, 1, d), lambda i, j, be, nv, na: (layer, be[blk(i, na)], 0, 0)),
        ],
        out_specs=pl.BlockSpec((MOE_ROWS * SUBLANES, LANES), lambda i, j, be, nv, na: (blk(i, na), 0)),
        scratch_shapes=[pltpu.VMEM((MOE_ROWS, d), BF16), pltpu.VMEM((MOE_ROWS, d), F32)],
    )
    return pl.pallas_call(
        _moe_kernel,
        grid_spec=grid_spec,
        out_shape=jax.ShapeDtypeStruct((n_slots * SUBLANES, LANES), U32),
        compiler_params=_cparams(("arbitrary", "arbitrary"), MOE_VMEM_LIMIT),
        name="moe",
    )(block_e, block_rows, n_active, slots, w_gate_up, w_gate_up,
      b_gate_up.reshape(depth, n_e, 2 * n_j, 1, MOE_FF_TILE),
      w_down, b_down.reshape(depth, n_e, 1, d))


def _combine_kernel(final_norm, dest_ref, dest_next_ref, gate_ref, x_ref, g2_ref, nf_ref, y_hbm, o_ref,
                    rows_scr, sems):
    i = pl.program_id(0)
    slot = i % 2

    def tile_copy(d_ref, buf, t, k):
        src = pl.multiple_of(d_ref[t * TOP_K + k] * SUBLANES, SUBLANES)
        dst = pl.multiple_of(t * SUBLANES, SUBLANES)
        return pltpu.make_async_copy(
            y_hbm.at[pl.ds(src, SUBLANES)], rows_scr.at[buf, k, pl.ds(dst, SUBLANES)], sems.at[buf])

    def start_tile(d_ref, buf):
        def issue(t, carry):
            for k in range(TOP_K):
                tile_copy(d_ref, buf, t, k).start(priority=k % 2)
            return carry
        lax.fori_loop(0, COMBINE_ROWS, issue, 0, unroll=2)

    @pl.when(i == 0)
    def _():
        start_tile(dest_ref, 0)

    has_next = i + 1 < pl.num_programs(0)
    for buf in range(2):
        @pl.when(has_next & (slot == 1 - buf))
        def _(buf=buf):
            start_tile(dest_next_ref, buf)

    for buf in range(2):
        @pl.when(slot == buf)
        def _(buf=buf):
            def drain(t, carry):
                for k in range(TOP_K):
                    tile_copy(dest_ref, buf, t, k).wait()
                return carry
            lax.fori_loop(0, COMBINE_ROWS, drain, 0, unroll=2)

    rows = rows_scr.at[slot]
    d = x_ref.shape[1]
    half = d // 2
    gates = gate_ref[...]
    gate_cols = [gates[:, k:k + 1] for k in range(TOP_K)]
    sumsq = None
    for c in range(SUBLANES):
        f_lo = None
        f_hi = None
        for k in range(TOP_K):
            lo, hi = _unpack_bf16_pair(_tile_chunk(rows.at[k], c, COMBINE_ROWS))
            f_lo = gate_cols[k] * lo if f_lo is None else f_lo + gate_cols[k] * lo
            f_hi = gate_cols[k] * hi if f_hi is None else f_hi + gate_cols[k] * hi
        sl_lo = slice(c * LANES, (c + 1) * LANES)
        sl_hi = slice(half + c * LANES, half + (c + 1) * LANES)
        y_lo = x_ref[:, sl_lo] + g2_ref[:, sl_lo] * f_lo
        y_hi = x_ref[:, sl_hi] + g2_ref[:, sl_hi] * f_hi
        o_ref[:, sl_lo] = y_lo
        o_ref[:, sl_hi] = y_hi
        if final_norm:
            part = jnp.sum(y_lo * y_lo + y_hi * y_hi, axis=-1, keepdims=True)
            sumsq = part if sumsq is None else sumsq + part
    if final_norm:
        o_ref[...] = o_ref[...] * lax.rsqrt(sumsq / d + EPS) * nf_ref[...]


def _combine(dest_flat, gates, x_all, mods, layer, group_of_tile_c, y_slots, n_tokens, norm_final, final_norm):
    d = x_all.shape[1]
    n_steps = n_tokens // COMBINE_ROWS
    return pl.pallas_call(
        functools.partial(_combine_kernel, final_norm),
        grid=(n_steps,),
        in_specs=[
            pl.BlockSpec((COMBINE_ROWS * TOP_K,), lambda i: (i,), memory_space=pltpu.SMEM),
            pl.BlockSpec((COMBINE_ROWS * TOP_K,), lambda i: (jnp.minimum(i + 1, n_steps - 1),),
                         memory_space=pltpu.SMEM),
            pl.BlockSpec((COMBINE_ROWS, LANES), lambda i: (i, 0)),
            pl.BlockSpec((COMBINE_ROWS, d), lambda i: (i, 0)),
            pl.BlockSpec((None, None, 1, d), lambda i: (layer, group_of_tile_c(i), 0, 5)),
            pl.BlockSpec((1, d), lambda i: (0, 0)),
            pl.BlockSpec(memory_space=pl.ANY),
        ],
        out_specs=pl.BlockSpec((COMBINE_ROWS, d), lambda i: (i, 0)),
        out_shape=jax.ShapeDtypeStruct((n_tokens, d), F32),
        scratch_shapes=[pltpu.VMEM((2, TOP_K, COMBINE_ROWS * SUBLANES, LANES), U32),
                        pltpu.SemaphoreType.DMA((2,))],
        compiler_params=_cparams(("arbitrary",)),
        name="combine",
    )(dest_flat, dest_flat, gates, x_all, mods, norm_final, y_slots)


def _pool_matrices(seg_len, tile):
    pos = jnp.arange(tile)
    seg = pos // seg_len
    p = pos % seg_len
    mats = []
    for w in POOL_WINDOWS:
        left = w // 2
        right = w - 1 - left
        lo = jnp.clip(p - left, 0, seg_len)
        hi = jnp.clip(p + right + 1, 0, seg_len)
        inside = ((seg[:, None] == seg[None, :]) & (p[None, :] >= lo[:, None]) & (p[None, :] < hi[:, None]))
        mats.append(jnp.where(inside, 1.0 / (hi - lo).astype(F32)[:, None], 0.0))
    return jnp.stack(mats).astype(BF16)


def _routing_plan(top_idx, rank, counts, n_blocks):
    experts = jnp.arange(N_EXPERTS, dtype=I32)
    blocks_e = (counts + MOE_ROWS - 1) // MOE_ROWS
    blk_end = jnp.cumsum(blocks_e)
    blk_start = blk_end - blocks_e
    start_e = blk_start * MOE_ROWS
    start_of = jnp.sum(jnp.where(top_idx[..., None] == experts, start_e, 0), axis=-1)
    dest = (start_of + rank).reshape(-1).astype(I32)
    blk = jnp.arange(n_blocks, dtype=I32)
    block_e = jnp.minimum(jnp.sum((blk[:, None] >= blk_end[None, :]).astype(I32), axis=-1), N_EXPERTS - 1)
    in_e = block_e[:, None] == experts
    local = blk - jnp.sum(jnp.where(in_e, blk_start, 0), axis=-1)
    rows_e = jnp.sum(jnp.where(in_e, counts, 0), axis=-1)
    block_rows = jnp.clip(rows_e - local * MOE_ROWS, 0, MOE_ROWS).astype(I32)
    n_active = blk_end[-1:].astype(I32)
    return dest, block_e.astype(I32), block_rows, n_active


def kernel(x, c, ctx, c_ctx, w_mod, b_mod, norm_mix, w_in, b_gate, head_gain, w_pool, pool_scale,
           w_out, norm_ffn, router_w, router_b, w_gate_up, b_gate_up, w_down, b_down, norm_final):
    b, s, d = x.shape
    ctx_len = ctx.shape[1]
    depth = w_mod.shape[0]
    n_lat = b * s
    n_ctx_rows = b * ctx_len
    t_all = n_lat + n_ctx_rows
    d_main = w_in.shape[2] - 4 * N_HEADS
    assert s % ROW_TILE == 0 and n_ctx_rows % ROW_TILE == 0 and ROW_TILE % ctx_len == 0
    assert ROW_TILE % GRID_W == 0 and s % CHUNK == 0 and ctx_len % CHUNK == 0
    assert b + 1 <= MOD_ROWS

    x_all = None
    c_rows = jnp.zeros((MOD_ROWS, d), F32).at[:b].set(c).at[b].set(c_ctx)
    mods = _modulation(c_rows, w_mod, b_mod).reshape(depth, MOD_ROWS, 1, N_MOD * d)

    tiles_per_batch = s // ROW_TILE
    group_of_tile = lambda i: jnp.minimum(i // tiles_per_batch, b)
    tile_kind = lambda i: (i >= n_lat // ROW_TILE).astype(I32)
    group_of_tile_c = lambda i: jnp.minimum(i // (s // COMBINE_ROWS), b)
    pmats = jnp.stack([_pool_matrices(GRID_W, ROW_TILE), _pool_matrices(ctx_len, ROW_TILE)])

    out = None
    for l in range(depth):
        last = l == depth - 1
        w_main = w_in[l, :, :d_main].astype(BF16)
        w_gate = jnp.pad(w_in[l, :, d_main:], ((0, 0), (0, LANES - 4 * N_HEADS))).astype(BF16)
        bias_row = jnp.pad(b_gate[l].reshape(1, 4 * N_HEADS), ((0, 0), (0, LANES - 4 * N_HEADS)))
        if x_all is None:
            p_main, gates, x_all = _in_proj(x.reshape(n_lat, d), ctx.reshape(n_ctx_rows, d), mods, l,
                                            group_of_tile, norm_mix[l].reshape(1, d), w_main, w_gate)
        else:
            p_main, gates = _in_proj(x_all, None, mods, l, group_of_tile, norm_mix[l].reshape(1, d),
                                     w_main, w_gate)
        hf, hb = _mlstm(p_main, gates, bias_row, b, s, ctx_len)
        rw = jnp.pad(router_w[l], ((0, 0), (0, LANES - N_EXPERTS))).astype(BF16)
        rb = jnp.pad(router_b[l].reshape(1, N_EXPERTS), ((0, 0), (0, LANES - N_EXPERTS)))
        x_mid, h2, logits = _mix_out(
            x_all, p_main, hf, hb, ctx_len, mods, l, group_of_tile, tile_kind,
            head_gain[l].reshape(1, -1), pmats, w_pool[l].astype(BF16), pool_scale[l].reshape(1, -1),
            w_out[l].astype(BF16), norm_ffn[l].reshape(1, d), rw, rb)

        n_tok = n_lat if last else t_all
        idx_pad, gate_pad, count_pad = _route(logits, n_tok)
        n_assign = n_tok * TOP_K
        n_blocks = -(-n_assign // MOE_ROWS) + N_EXPERTS
        dest, block_e, block_rows, n_active = _routing_plan(
            idx_pad[:, :TOP_K], idx_pad[:, TOP_K:2 * TOP_K], count_pad[0, :N_EXPERTS], n_blocks)
        slots = _dispatch(dest, h2, n_tok, n_blocks * MOE_ROWS)
        y_slots = _moe(slots, block_e, block_rows, n_active, l, w_gate_up, b_gate_up, w_down, b_down)
        x_next = _combine(dest, gate_pad, x_mid, mods, l, group_of_tile_c, y_slots, n_tok,
                          norm_final.reshape(1, d), last)
        if last:
            out = x_next.reshape(b, s, d)
        else:
            x_all = x_next
    return out
```
